```python
import jax, jax.numpy as jnp
from jax import lax
import numpy as np

D_MODEL = 2048
BATCH = 4
SEQ = 4096
DEPTH = 4

GRID_W = 64
ATTN_WIDTH = D_MODEL // 2
CONV_CH = D_MODEL - ATTN_WIDTH
HEAD_DIM = 128
N_Q_HEADS = ATTN_WIDTH // HEAD_DIM
N_KV_HEADS = 2
GQA_GROUP = N_Q_HEADS // N_KV_HEADS
KV_WIDTH = N_KV_HEADS * HEAD_DIM
W_IN_COLS = ATTN_WIDTH + 2 * KV_WIDTH + 2 * CONV_CH
ROPE_THETA = 10000.0
ROPE_AXIS_DIM = HEAD_DIM // 2
Q_BLOCK = 128
CONV_K = 31
FF_DIM = 5632
N_EXPERTS = 8
TOP_K = 2
EXPERT_BLOCK = 128
N_DENSE = (DEPTH + 1) // 2
N_MOE = DEPTH // 2
ALPHA = (2.0 * DEPTH) ** 0.25
BETA = (8.0 * DEPTH) ** -0.25
LN_EPS = 1e-5
RMS_EPS = 1e-6

kernel_name = "hybrid_conv_gqa_moe_deepnorm_encoder"


def layer_norm(x, g, b):
    xf = x.astype(jnp.float32)
    mu = jnp.mean(xf, axis=-1, keepdims=True)
    var = jnp.mean(jnp.square(xf - mu), axis=-1, keepdims=True)
    return ((xf - mu) * lax.rsqrt(var + LN_EPS) * g.astype(jnp.float32) + b.astype(jnp.float32)).astype(x.dtype)


def rms_norm(x, g):
    xf = x.astype(jnp.float32)
    ms = jnp.mean(jnp.square(xf), axis=-1, keepdims=True)
    return (xf * lax.rsqrt(ms + RMS_EPS) * g.astype(jnp.float32)).astype(x.dtype)


def rope_1d(x, pos):
    inv_freq = ROPE_THETA ** (-jnp.arange(0, ROPE_AXIS_DIM, 2, dtype=jnp.float32) / ROPE_AXIS_DIM)
    ang = pos.astype(jnp.float32)[:, None] * inv_freq[None, :]
    cos = jnp.cos(ang)[None, :, None, :]
    sin = jnp.sin(ang)[None, :, None, :]
    xf = x.astype(jnp.float32)
    x1, x2 = jnp.split(xf, 2, axis=-1)
    return jnp.concatenate([x1 * cos - x2 * sin, x1 * sin + x2 * cos], axis=-1).astype(x.dtype)


def axial_rope(x, row, col):
    return jnp.concatenate([rope_1d(x[..., :ROPE_AXIS_DIM], row),
                            rope_1d(x[..., ROPE_AXIS_DIM:], col)], axis=-1)


def block_attention(q, k, v):
    B, S = q.shape[0], q.shape[1]
    n_blk = S // Q_BLOCK
    qb = q.reshape(B, n_blk, Q_BLOCK, N_KV_HEADS, GQA_GROUP, HEAD_DIM)
    qb = jnp.moveaxis(qb, 1, 0)
    scale = HEAD_DIM ** -0.5

    def one_block(q_blk):
        s = jnp.einsum('bqhgd,bkhd->bhgqk', q_blk, k,
                       preferred_element_type=jnp.float32) * scale
        p = jax.nn.softmax(s, axis=-1)
        return jnp.einsum('bhgqk,bkhd->bqhgd', p.astype(v.dtype), v)

    o = lax.map(one_block, qb)
    return jnp.moveaxis(o, 0, 1).reshape(B, S, ATTN_WIDTH)


def conformer_conv(u, w_dw, b_dw, g, b):
    a, gate = jnp.split(u, 2, axis=-1)
    h = a * jax.nn.sigmoid(gate)
    h = lax.conv_general_dilated(
        h, w_dw[:, None, :], window_strides=(1,),
        padding=[(CONV_K // 2, CONV_K // 2)],
        dimension_numbers=('NWC', 'WIO', 'NWC'),
        feature_group_count=CONV_CH) + b_dw
    h = layer_norm(h, g, b)
    return jax.nn.silu(h)


def token_mixer(x, w_in, q_g, k_g, conv_w, conv_b, conv_g, conv_bt, w_out, row, col):
    B, S, _ = x.shape
    proj = x @ w_in
    q, k, v, u = jnp.split(proj, [ATTN_WIDTH, ATTN_WIDTH + KV_WIDTH, ATTN_WIDTH + 2 * KV_WIDTH], axis=-1)
    q = axial_rope(rms_norm(q.reshape(B, S, N_Q_HEADS, HEAD_DIM), q_g), row, col)
    k = axial_rope(rms_norm(k.reshape(B, S, N_KV_HEADS, HEAD_DIM), k_g), row, col)
    v = v.reshape(B, S, N_KV_HEADS, HEAD_DIM)
    attn = block_attention(q, k, v)
    conv = conformer_conv(u, conv_w, conv_b, conv_g, conv_bt)
    return jnp.concatenate([attn, conv], axis=-1) @ w_out


def swiglu(x, w_up, w_down):
    g, u = jnp.split(x @ w_up, 2, axis=-1)
    return (jax.nn.silu(g) * u) @ w_down


def moe_ffn(x, w_router, w_up, w_down):
    B, S, D = x.shape
    T = B * S
    xf = x.reshape(T, D)
    logits = (xf @ w_router).astype(jnp.float32)
    top_val, top_idx = lax.top_k(logits, TOP_K)
    gates = jax.nn.softmax(top_val, axis=-1)
    A = T * TOP_K
    eid = top_idx.reshape(A).astype(jnp.int32)
    tok = jnp.repeat(jnp.arange(T, dtype=jnp.int32), TOP_K)
    gate = gates.reshape(A)
    order = jnp.argsort(eid)
    eid_s, tok_s, gate_s = eid[order], tok[order], gate[order]
    counts = jnp.bincount(eid, length=N_EXPERTS).astype(jnp.int32)
    starts = jnp.cumsum(counts) - counts
    padded = (counts + EXPERT_BLOCK - 1) // EXPERT_BLOCK * EXPERT_BLOCK
    pad_ends = jnp.cumsum(padded)
    pad_starts = pad_ends - padded
    dest = pad_starts[eid_s] + jnp.arange(A, dtype=jnp.int32) - starts[eid_s]
    n_blk = -(-A // EXPERT_BLOCK) + N_EXPERTS
    P = n_blk * EXPERT_BLOCK
    buf_tok = jnp.full((P,), T, jnp.int32).at[dest].set(tok_s)
    buf_gate = jnp.zeros((P,), jnp.float32).at[dest].set(gate_s)
    blk_start = jnp.arange(n_blk, dtype=jnp.int32) * EXPERT_BLOCK
    blk_exp = jnp.minimum(jnp.searchsorted(pad_ends, blk_start, side='right'), N_EXPERTS - 1)
    x_pad = jnp.concatenate([xf, jnp.zeros((1, D), xf.dtype)], axis=0)

    def expert_block(args):
        tok_b, e = args
        return swiglu(x_pad[tok_b], w_up[e], w_down[e])

    y = lax.map(expert_block, (buf_tok.reshape(n_blk, EXPERT_BLOCK), blk_exp))
    y = y.reshape(P, D) * buf_gate[:, None].astype(y.dtype)
    out = jnp.zeros((T + 1, D), y.dtype).at[buf_tok].add(y)[:T]
    return out.reshape(B, S, D)


def setup_inputs(seed: int = 0) -> dict:
    key = jax.random.key(seed)
    ks = jax.random.split(key, 20)
    f32 = jnp.float32

    def nrm(k, shape, scale):
        return jax.random.normal(k, shape, f32) * scale

    def gain(k, shape):
        return 1.0 + 0.01 * jax.random.normal(k, shape, f32)

    v0 = ATTN_WIDTH + KV_WIDTH
    c0 = ATTN_WIDTH + 2 * KV_WIDTH
    col_scale = jnp.ones((W_IN_COLS,), f32).at[v0:v0 + KV_WIDTH].set(BETA).at[c0:c0 + CONV_CH].set(BETA)
    w_in = nrm(ks[2], (DEPTH, D_MODEL, W_IN_COLS), D_MODEL ** -0.5) * col_scale

    return {
        "x": jax.random.normal(ks[0], (BATCH, SEQ, D_MODEL), f32),
        "ln_in_g": gain(ks[1], (D_MODEL,)),
        "ln_in_b": nrm(ks[19], (D_MODEL,), 0.01),
        "w_in": w_in,
        "q_norm_g": gain(ks[3], (DEPTH, HEAD_DIM)),
        "k_norm_g": gain(ks[4], (DEPTH, HEAD_DIM)),
        "conv_w": nrm(ks[5], (DEPTH, CONV_K, CONV_CH), CONV_K ** -0.5),
        "conv_b": nrm(ks[6], (DEPTH, CONV_CH), 0.01),
        "conv_ln_g": gain(ks[7], (DEPTH, CONV_CH)),
        "conv_ln_b": nrm(ks[8], (DEPTH, CONV_CH), 0.01),
        "w_out": nrm(ks[9], (DEPTH, D_MODEL, D_MODEL), BETA * D_MODEL ** -0.5),
        "ln_mix_g": gain(ks[10], (DEPTH, D_MODEL)),
        "ln_mix_b": nrm(ks[11], (DEPTH, D_MODEL), 0.01),
        "ffn_w_up": nrm(ks[12], (N_DENSE, D_MODEL, 2 * FF_DIM), D_MODEL ** -0.5),
        "ffn_w_down": nrm(ks[13], (N_DENSE, FF_DIM, D_MODEL), BETA * FF_DIM ** -0.5),
        "moe_router": nrm(ks[14], (N_MOE, D_MODEL, N_EXPERTS), D_MODEL ** -0.5),
        "moe_w_up": nrm(ks[15], (N_MOE, N_EXPERTS, D_MODEL, 2 * FF_DIM), D_MODEL ** -0.5),
        "moe_w_down": nrm(ks[16], (N_MOE, N_EXPERTS, FF_DIM, D_MODEL), BETA * FF_DIM ** -0.5),
        "ln_ffn_g": gain(ks[17], (DEPTH, D_MODEL)),
        "ln_ffn_b": nrm(ks[18], (DEPTH, D_MODEL), 0.01),
    }


def reference(x, ln_in_g, ln_in_b, w_in, q_norm_g, k_norm_g, conv_w, conv_b, conv_ln_g, conv_ln_b,
              w_out, ln_mix_g, ln_mix_b, ffn_w_up, ffn_w_down, moe_router, moe_w_up, moe_w_down,
              ln_ffn_g, ln_ffn_b):
    S = x.shape[1]
    rows = S // GRID_W
    row = jnp.repeat(jnp.arange(rows, dtype=jnp.int32), GRID_W)
    col = jnp.tile(jnp.arange(GRID_W, dtype=jnp.int32), rows)
    h = layer_norm(x, ln_in_g, ln_in_b)
    for i in range(DEPTH):
        y = token_mixer(h, w_in[i], q_norm_g[i], k_norm_g[i], conv_w[i], conv_b[i],
                        conv_ln_g[i], conv_ln_b[i], w_out[i], row, col)
        h = layer_norm(ALPHA * h + y, ln_mix_g[i], ln_mix_b[i])
        if i % 2 == 0:
            y = swiglu(h, ffn_w_up[i // 2], ffn_w_down[i // 2])
        else:
            y = moe_ffn(h, moe_router[i // 2], moe_w_up[i // 2], moe_w_down[i // 2])
        h = layer_norm(ALPHA * h + y, ln_ffn_g[i], ln_ffn_b[i])
    return h
```

```python
import functools

import jax
import jax.numpy as jnp
from jax import lax
from jax.experimental import pallas as pl
from jax.experimental.pallas import tpu as pltpu

F32 = jnp.float32
BF16 = jnp.bfloat16

D_MODEL = 2048
DEPTH = 4
GRID_W = 64
ATTN_WIDTH = 1024
CONV_CH = 1024
HEAD_DIM = 128
N_Q_HEADS = 8
N_KV_HEADS = 2
GQA_GROUP = 4
KV_WIDTH = 256
QKV_COLS = ATTN_WIDTH + 2 * KV_WIDTH
ROPE_THETA = 10000.0
ROPE_AXIS_DIM = 64
CONV_K = 31
CONV_HALO = 16
FF_DIM = 5632
N_EXPERTS = 8
ALPHA = (2.0 * DEPTH) ** 0.25
LN_EPS = 1e-5
RMS_EPS = 1e-6

V7X_VMEM_BYTES = 64 * 1024 * 1024
VMEM_LIMIT = V7X_VMEM_BYTES - 8 * 1024 * 1024
LANES = 128

TM = 512
TM_UP = 1024
TN = 512
TQ = 512
TS_CONV = 256
RC_CONV = 64
TM_MOE = 512
TK_DOWN = 512


def _params(sem):
    return pltpu.CompilerParams(dimension_semantics=sem, vmem_limit_bytes=VMEM_LIMIT)


def _layer_norm(z, g, b):
    mu = jnp.mean(z, axis=-1, keepdims=True)
    zc = z - mu
    var = jnp.mean(zc * zc, axis=-1, keepdims=True)
    return zc * lax.rsqrt(var + LN_EPS) * g + b


def _ln_in_kernel(x_ref, g_ref, b_ref, h_ref, hb_ref):
    y = _layer_norm(x_ref[...], g_ref[...], b_ref[...])
    h_ref[...] = y
    hb_ref[...] = y.astype(BF16)


def _ln_in(x2, g, b):
    T, D = x2.shape
    row = pl.BlockSpec((TM, D), lambda i: (i, 0))
    vec = pl.BlockSpec((1, D), lambda i: (0, 0))
    return pl.pallas_call(
        _ln_in_kernel,
        grid=(T // TM,),
        in_specs=[row, vec, vec],
        out_specs=[row, row],
        out_shape=[jax.ShapeDtypeStruct((T, D), F32), jax.ShapeDtypeStruct((T, D), BF16)],
        compiler_params=_params(("parallel",)),
        name="ln_in",
    )(x2, g.reshape(1, D), b.reshape(1, D))


def _qkv_kernel(a_ref, w_ref, cos_ref, sin_ref, qg_ref, kg_ref, o_ref, wb_ref):
    j = pl.program_id(0)
    i = pl.program_id(1)

    @pl.when(i == 0)
    def _():
        wb_ref[...] = w_ref[...].astype(BF16)

    acc = jnp.dot(a_ref[...], wb_ref[...], preferred_element_type=F32)
    cos = cos_ref[...]
    sin = sin_ref[...]
    lane = lax.broadcasted_iota(jnp.int32, (TM, HEAD_DIM), 1)
    first_half = (lane & (ROPE_AXIS_DIM // 2)) == 0

    def norm_rope(xc, g, scale):
        ms = jnp.mean(xc * xc, axis=-1, keepdims=True)
        xn = xc * lax.rsqrt(ms + RMS_EPS) * g
        sw = jnp.where(first_half, pltpu.roll(xn, HEAD_DIM - 32, 1), pltpu.roll(xn, 32, 1))
        return (xn * cos + sw * sin) * scale

    n_chunks = TN // HEAD_DIM
    q_scale = HEAD_DIM ** -0.5

    @pl.when(j < ATTN_WIDTH // TN)
    def _():
        for c in range(n_chunks):
            cs = slice(c * HEAD_DIM, (c + 1) * HEAD_DIM)
            o_ref[:, cs] = norm_rope(acc[:, cs], qg_ref[...], q_scale).astype(BF16)

    @pl.when(j >= ATTN_WIDTH // TN)
    def _():
        for c in range(n_chunks):
            cs = slice(c * HEAD_DIM, (c + 1) * HEAD_DIM)
            if c < N_KV_HEADS:
                o_ref[:, cs] = norm_rope(acc[:, cs], kg_ref[...], 1.0).astype(BF16)
            else:
                o_ref[:, cs] = acc[:, cs].astype(BF16)


def _qkv_proj(hb, w_in, layer, cos_t, sin_t, q_g, k_g, S):
    T, D = hb.shape
    assert TN == 2 * KV_WIDTH and ATTN_WIDTH % TN == 0 and S % TM == 0
    s_tiles = S // TM
    return pl.pallas_call(
        _qkv_kernel,
        grid=(QKV_COLS // TN, T // TM),
        in_specs=[
            pl.BlockSpec((TM, D), lambda j, i: (i, 0)),
            pl.BlockSpec((None, D, TN), lambda j, i: (layer, 0, j)),
            pl.BlockSpec((TM, HEAD_DIM), lambda j, i: (i % s_tiles, 0)),
            pl.BlockSpec((TM, HEAD_DIM), lambda j, i: (i % s_tiles, 0)),
            pl.BlockSpec((None, 1, HEAD_DIM), lambda j, i: (layer, 0, 0)),
            pl.BlockSpec((None, 1, HEAD_DIM), lambda j, i: (layer, 0, 0)),
        ],
        out_specs=pl.BlockSpec((TM, TN), lambda j, i: (i, j)),
        out_shape=jax.ShapeDtypeStruct((T, QKV_COLS), BF16),
        scratch_shapes=[pltpu.VMEM((D, TN), BF16)],
        compiler_params=_params(("arbitrary", "arbitrary")),
        name="qkv_proj",
    )(hb, w_in, cos_t, sin_t, q_g, k_g)


def _glu_kernel(a_ref, wa_ref, wg_ref, o_ref, wab_ref, wgb_ref):
    i = pl.program_id(1)

    @pl.when(i == 0)
    def _():
        wab_ref[...] = wa_ref[...].astype(BF16)
        wgb_ref[...] = wg_ref[...].astype(BF16)

    a = a_ref[...]
    val = jnp.dot(a, wab_ref[...], preferred_element_type=F32)
    gate = jnp.dot(a, wgb_ref[...], preferred_element_type=F32)
    o_ref[...] = val * jax.nn.sigmoid(gate)


def _glu_proj(hb, w_in, layer):
    T, D = hb.shape
    val0 = QKV_COLS // TN
    gate0 = (QKV_COLS + CONV_CH) // TN
    return pl.pallas_call(
        _glu_kernel,
        grid=(CONV_CH // TN, T // TM),
        in_specs=[
            pl.BlockSpec((TM, D), lambda j, i: (i, 0)),
            pl.BlockSpec((None, D, TN), lambda j, i: (layer, 0, val0 + j)),
            pl.BlockSpec((None, D, TN), lambda j, i: (layer, 0, gate0 + j)),
        ],
        out_specs=pl.BlockSpec((TM, TN), lambda j, i: (i, j)),
        out_shape=jax.ShapeDtypeStruct((T, CONV_CH), F32),
        scratch_shapes=[pltpu.VMEM((D, TN), BF16), pltpu.VMEM((D, TN), BF16)],
        compiler_params=_params(("arbitrary", "arbitrary")),
        name="glu_proj",
    )(hb, w_in, w_in)


def _conv_kernel(cur_ref, prev_ref, next_ref, w_ref, cb_ref, g_ref, b_ref, o_ref, ext_ref, y_ref):
    s = pl.program_id(1)
    ts = cur_ref.shape[1]
    has_prev = (s > 0).astype(F32)
    has_next = (s < pl.num_programs(1) - 1).astype(F32)
    ext_ref[0:CONV_HALO, :] = prev_ref[0] * has_prev
    ext_ref[CONV_HALO:CONV_HALO + ts, :] = cur_ref[0]
    ext_ref[CONV_HALO + ts:, :] = next_ref[0] * has_next
    off = CONV_HALO - CONV_K // 2
    for c in range(CONV_CH // LANES):
        cs = slice(c * LANES, (c + 1) * LANES)
        for r in range(ts // RC_CONV):
            r0 = r * RC_CONV
            acc = jnp.broadcast_to(cb_ref[:, cs], (RC_CONV, LANES))
            for k in range(CONV_K):
                acc = acc + ext_ref[r0 + off + k:r0 + off + k + RC_CONV, cs] * w_ref[k:k + 1, cs]
            y_ref[r0:r0 + RC_CONV, cs] = acc
    y = _layer_norm(y_ref[...], g_ref[...], b_ref[...])
    o_ref[0] = (y * jax.nn.sigmoid(y)).astype(BF16)


def _conv_module(u3, conv_w, conv_b, conv_g, conv_bt, layer):
    B, S, C = u3.shape
    ts = TS_CONV
    assert S % ts == 0 and ts % CONV_HALO == 0 and ts % RC_CONV == 0
    halo_per_tile = ts // CONV_HALO
    n_halo = S // CONV_HALO
    vec = pl.BlockSpec((None, 1, C), lambda b, s: (layer, 0, 0))
    return pl.pallas_call(
        _conv_kernel,
        grid=(B, S // ts),
        in_specs=[
            pl.BlockSpec((1, ts, C), lambda b, s: (b, s, 0)),
            pl.BlockSpec((1, CONV_HALO, C), lambda b, s: (b, jnp.maximum(s * halo_per_tile - 1, 0), 0)),
            pl.BlockSpec((1, CONV_HALO, C), lambda b, s: (b, jnp.minimum((s + 1) * halo_per_tile, n_halo - 1), 0)),
            pl.BlockSpec((None, CONV_K, C), lambda b, s: (layer, 0, 0)),
            vec, vec, vec,
        ],
        out_specs=pl.BlockSpec((1, ts, C), lambda b, s: (b, s, 0)),
        out_shape=jax.ShapeDtypeStruct((B, S, C), BF16),
        scratch_shapes=[pltpu.VMEM((ts + 2 * CONV_HALO, C), F32), pltpu.VMEM((ts, C), F32)],
        compiler_params=_params(("parallel", "parallel")),
        name="conv_module",
    )(u3, u3, u3, conv_w, conv_b, conv_g, conv_bt)


def _attn_kernel(q_ref, k_ref, v_ref, o_ref):
    q = q_ref[0]
    k = k_ref[0]
    v = v_ref[0]
    s = lax.dot_general(q, k, (((1,), (1,)), ((), ())), preferred_element_type=F32)
    m = jnp.max(s, axis=-1, keepdims=True)
    p = jnp.exp(s - m)
    l = jnp.sum(p, axis=-1, keepdims=True)
    o = jnp.dot(p.astype(BF16), v, preferred_element_type=F32)
    o_ref[0] = (o / l).astype(BF16)


def _attention(qkv3):
    B, S, _ = qkv3.shape
    k0 = ATTN_WIDTH // HEAD_DIM
    v0 = (ATTN_WIDTH + KV_WIDTH) // HEAD_DIM
    return pl.pallas_call(
        _attn_kernel,
        grid=(B, N_Q_HEADS, S // TQ),
        in_specs=[
            pl.BlockSpec((1, TQ, HEAD_DIM), lambda b, h, qi: (b, qi, h)),
            pl.BlockSpec((1, S, HEAD_DIM), lambda b, h, qi: (b, 0, k0 + h // GQA_GROUP)),
            pl.BlockSpec((1, S, HEAD_DIM), lambda b, h, qi: (b, 0, v0 + h // GQA_GROUP)),
        ],
        out_specs=pl.BlockSpec((1, TQ, HEAD_DIM), lambda b, h, qi: (b, qi, h)),
        out_shape=jax.ShapeDtypeStruct((B, S, ATTN_WIDTH), BF16),
        compiler_params=_params(("parallel", "parallel", "parallel")),
        name="attention",
    )(qkv3, qkv3, qkv3)


def _out_proj_kernel(attn_ref, conv_ref, w_ref, h_ref, g_ref, b_ref, ho_ref, hbo_ref):
    y = jnp.dot(attn_ref[...], w_ref[0:ATTN_WIDTH, :], preferred_element_type=F32)
    y = y + jnp.dot(conv_ref[...], w_ref[ATTN_WIDTH:, :], preferred_element_type=F32)
    z = _layer_norm(ALPHA * h_ref[...] + y, g_ref[...], b_ref[...])
    ho_ref[...] = z
    hbo_ref[...] = z.astype(BF16)


def _out_proj(attn, conv, w_out_b, h, g, b, layer):
    T, D = h.shape
    row = pl.BlockSpec((TM, D), lambda i: (i, 0))
    half = pl.BlockSpec((TM, ATTN_WIDTH), lambda i: (i, 0))
    vec = pl.BlockSpec((None, 1, D), lambda i: (layer, 0, 0))
    return pl.pallas_call(
        _out_proj_kernel,
        grid=(T // TM,),
        in_specs=[half, half, pl.BlockSpec((None, D, D), lambda i: (layer, 0, 0)), row, vec, vec],
        out_specs=[row, row],
        out_shape=[jax.ShapeDtypeStruct((T, D), F32), jax.ShapeDtypeStruct((T, D), BF16)],
        compiler_params=_params(("parallel",)),
        name="out_proj",
    )(attn, conv, w_out_b, h, g, b)


def _ffn_up_kernel(a_ref, wg_ref, wu_ref, o_ref, wgb_ref, wub_ref):
    i = pl.program_id(1)

    @pl.when(i == 0)
    def _():
        wgb_ref[...] = wg_ref[...].astype(BF16)
        wub_ref[...] = wu_ref[...].astype(BF16)

    a = a_ref[...]
    g = jnp.dot(a, wgb_ref[...], preferred_element_type=F32)
    u = jnp.dot(a, wub_ref[...], preferred_element_type=F32)
    o_ref[...] = (g * jax.nn.sigmoid(g) * u).astype(BF16)


def _ffn_up(hb, w_up, layer):
    T, D = hb.shape
    nf = FF_DIM // TN
    return pl.pallas_call(
        _ffn_up_kernel,
        grid=(nf, T // TM_UP),
        in_specs=[
            pl.BlockSpec((TM_UP, D), lambda f, i: (i, 0)),
            pl.BlockSpec((None, D, TN), lambda f, i: (layer, 0, f)),
            pl.BlockSpec((None, D, TN), lambda f, i: (layer, 0, nf + f)),
        ],
        out_specs=pl.BlockSpec((TM_UP, TN), lambda f, i: (i, f)),
        out_shape=jax.ShapeDtypeStruct((T, FF_DIM), BF16),
        scratch_shapes=[pltpu.VMEM((D, TN), BF16), pltpu.VMEM((D, TN), BF16)],
        compiler_params=_params(("arbitrary", "arbitrary")),
        name="ffn_up",
    )(hb, w_up, w_up)


def _ffn_down_kernel(a_ref, w_ref, h_ref, g_ref, b_ref, ho_ref, hbo_ref, acc_ref):
    k = pl.program_id(1)
    part = jnp.dot(a_ref[...], w_ref[...], preferred_element_type=F32)

    @pl.when(k == 0)
    def _():
        acc_ref[...] = part

    @pl.when(k > 0)
    def _():
        acc_ref[...] += part

    @pl.when(k == pl.num_programs(1) - 1)
    def _():
        z = _layer_norm(ALPHA * h_ref[...] + acc_ref[...], g_ref[...], b_ref[...])
        ho_ref[...] = z
        hbo_ref[...] = z.astype(BF16)


def _ffn_down(act, w_down_b, h, g, b, dense_layer, layer):
    T, D = h.shape
    row = pl.BlockSpec((TM, D), lambda i, k: (i, 0))
    vec = pl.BlockSpec((None, 1, D), lambda i, k: (layer, 0, 0))
    return pl.pallas_call(
        _ffn_down_kernel,
        grid=(T // TM, FF_DIM // TK_DOWN),
        in_specs=[
            pl.BlockSpec((TM, TK_DOWN), lambda i, k: (i, k)),
            pl.BlockSpec((None, TK_DOWN, D), lambda i, k: (dense_layer, k, 0)),
            row, vec, vec,
        ],
        out_specs=[row, row],
        out_shape=[jax.ShapeDtypeStruct((T, D), F32), jax.ShapeDtypeStruct((T, D), BF16)],
        scratch_shapes=[pltpu.VMEM((TM, D), F32)],
        compiler_params=_params(("parallel", "arbitrary")),
        name="ffn_down",
    )(act, w_down_b, h, g, b)


_L_E1, _L_E2, _L_R1, _L_R2, _L_G1, _L_G2 = range(6)


def _router_kernel(h_ref, wr_ref, info_ref, cnt_ref, carry_ref):
    i = pl.program_id(0)

    @pl.when(i == 0)
    def _():
        carry_ref[...] = jnp.zeros_like(carry_ref)

    logits = jnp.dot(h_ref[...], wr_ref[...], preferred_element_type=F32, precision=lax.Precision.HIGHEST)
    lane = lax.broadcasted_iota(jnp.int32, logits.shape, 1).astype(F32)
    neg = jnp.float32(-jnp.inf)
    lg = jnp.where(lane < N_EXPERTS, logits, neg)
    m1 = jnp.max(lg, axis=-1, keepdims=True)
    i1 = jnp.min(jnp.where(lg == m1, lane, float(LANES)), axis=-1, keepdims=True)
    oh1 = lane == i1
    lg2 = jnp.where(oh1, neg, lg)
    m2 = jnp.max(lg2, axis=-1, keepdims=True)
    i2 = jnp.min(jnp.where(lg2 == m2, lane, float(LANES)), axis=-1, keepdims=True)
    oh2 = lane == i2
    e = jnp.exp(m2 - m1)
    g1 = 1.0 / (1.0 + e)
    g2 = e * g1
    cnt = oh1.astype(F32) + oh2.astype(F32)
    tm = cnt.shape[0]
    row = lax.broadcasted_iota(jnp.int32, (tm, tm), 0)
    col = lax.broadcasted_iota(jnp.int32, (tm, tm), 1)
    strict_lower = jnp.where(col < row, 1.0, 0.0).astype(BF16)
    before = jnp.dot(strict_lower, cnt.astype(BF16), preferred_element_type=F32) + carry_ref[...]
    r1 = jnp.sum(jnp.where(oh1, before, 0.0), axis=-1, keepdims=True)
    r2 = jnp.sum(jnp.where(oh2, before, 0.0), axis=-1, keepdims=True)
    carry_ref[...] += jnp.sum(cnt, axis=0, keepdims=True)
    cnt_ref[...] = jnp.broadcast_to(carry_ref[...], cnt_ref.shape)
    info = jnp.zeros_like(logits)
    for idx, val in ((_L_E1, i1), (_L_E2, i2), (_L_R1, r1), (_L_R2, r2), (_L_G1, g1), (_L_G2, g2)):
        info = jnp.where(lane == idx, val, info)
    info_ref[...] = info


def _router(h, w_router_pad):
    T, D = h.shape
    return pl.pallas_call(
        _router_kernel,
        grid=(T // TM,),
        in_specs=[pl.BlockSpec((TM, D), lambda i: (i, 0)), pl.BlockSpec((D, LANES), lambda i: (0, 0))],
        out_specs=[pl.BlockSpec((TM, LANES), lambda i: (i, 0)), pl.BlockSpec((8, LANES), lambda i: (0, 0))],
        out_shape=[jax.ShapeDtypeStruct((T, LANES), F32), jax.ShapeDtypeStruct((8, LANES), F32)],
        scratch_shapes=[pltpu.VMEM((1, LANES), F32)],
        compiler_params=_params(("arbitrary",)),
        name="moe_router",
    )(h, w_router_pad)


def _row_copy(src_hbm, src_row, dst, dst_row, sem):
    return pltpu.make_async_copy(src_hbm.at[pl.ds(src_row, 1), :], dst.at[pl.ds(dst_row, 1), :], sem)


def _dispatch_kernel(nused_ref, tok_ref, h_hbm, o_ref, buf_ref, sem):
    i = pl.program_id(0)
    rows = buf_ref.shape[0]

    @pl.when(i < nused_ref[0])
    def _():
        def issue(r, c):
            _row_copy(h_hbm, tok_ref[0, 0, r], buf_ref, r, sem).start()
            return c

        lax.fori_loop(0, rows, issue, 0, unroll=8)

        def wait(r, c):
            _row_copy(h_hbm, 0, buf_ref, r, sem).wait()
            return c

        lax.fori_loop(0, rows, wait, 0, unroll=8)
        o_ref[...] = buf_ref[...].astype(BF16)

    @pl.when(i >= nused_ref[0])
    def _():
        o_ref[...] = jnp.zeros_like(o_ref)


def _dispatch(h, buf_tok3, nused):
    T, D = h.shape
    n_blk = buf_tok3.shape[0]
    grid_spec = pltpu.PrefetchScalarGridSpec(
        num_scalar_prefetch=1,
        grid=(n_blk,),
        in_specs=[
            pl.BlockSpec((1, 1, TM_MOE), lambda i, nu: (i, 0, 0), memory_space=pltpu.SMEM),
            pl.BlockSpec(memory_space=pl.ANY),
        ],
        out_specs=pl.BlockSpec((TM_MOE, D), lambda i, nu: (i, 0)),
        scratch_shapes=[pltpu.VMEM((TM_MOE, D), F32), pltpu.SemaphoreType.DMA(())],
    )
    return pl.pallas_call(
        _dispatch_kernel,
        grid_spec=grid_spec,
        out_shape=jax.ShapeDtypeStruct((n_blk * TM_MOE, D), BF16),
        compiler_params=_params(("arbitrary",)),
        name="moe_dispatch",
    )(nused, buf_tok3, h)


def _expert_changed(be_ref, i):
    return jnp.logical_or(i == 0, be_ref[i] != be_ref[jnp.maximum(i - 1, 0)])


def _moe_up_kernel(be_ref, nused_ref, x_ref, wg_ref, wu_ref, o_ref, wgb_ref, wub_ref):
    i = pl.program_id(1)

    @pl.when(_expert_changed(be_ref, i))
    def _():
        wgb_ref[...] = wg_ref[...].astype(BF16)
        wub_ref[...] = wu_ref[...].astype(BF16)

    @pl.when(i < nused_ref[0])
    def _():
        x = x_ref[...]
        g = jnp.dot(x, wgb_ref[...], preferred_element_type=F32)
        u = jnp.dot(x, wub_ref[...], preferred_element_type=F32)
        o_ref[...] = (g * jax.nn.sigmoid(g) * u).astype(BF16)

    @pl.when(i >= nused_ref[0])
    def _():
        o_ref[...] = jnp.zeros_like(o_ref)


def _moe_up(xs, w_up, moe_layer, blk_exp, nused):
    P, D = xs.shape
    nf = FF_DIM // TN
    n_blk = P // TM_MOE
    grid_spec = pltpu.PrefetchScalarGridSpec(
        num_scalar_prefetch=2,
        grid=(nf, n_blk),
        in_specs=[
            pl.BlockSpec((TM_MOE, D), lambda f, i, be, nu: (jnp.minimum(i, nu[0] - 1), 0)),
            pl.BlockSpec((None, None, D, TN), lambda f, i, be, nu: (moe_layer, be[i], 0, f)),
            pl.BlockSpec((None, None, D, TN), lambda f, i, be, nu: (moe_layer, be[i], 0, nf + f)),
        ],
        out_specs=pl.BlockSpec((TM_MOE, TN), lambda f, i, be, nu: (i, f)),
        scratch_shapes=[pltpu.VMEM((D, TN), BF16), pltpu.VMEM((D, TN), BF16)],
    )
    return pl.pallas_call(
        _moe_up_kernel,
        grid_spec=grid_spec,
        out_shape=jax.ShapeDtypeStruct((P, FF_DIM), BF16),
        compiler_params=_params(("arbitrary", "arbitrary")),
        name="moe_up",
    )(blk_exp, nused, xs, w_up, w_up)


def _moe_down_kernel(be_ref, nused_ref, a_ref, w_ref, o_ref, wb_ref):
    i = pl.program_id(1)

    @pl.when(_expert_changed(be_ref, i))
    def _():
        wb_ref[...] = w_ref[...].astype(BF16)

    @pl.when(i < nused_ref[0])
    def _():
        o_ref[...] = jnp.dot(a_ref[...], wb_ref[...], preferred_element_type=F32)

    @pl.when(i >= nused_ref[0])
    def _():
        o_ref[...] = jnp.zeros_like(o_ref)


def _moe_down(act, w_down, moe_layer, blk_exp, nused):
    P, F = act.shape
    n_blk = P // TM_MOE
    grid_spec = pltpu.PrefetchScalarGridSpec(
        num_scalar_prefetch=2,
        grid=(D_MODEL // TN, n_blk),
        in_specs=[
            pl.BlockSpec((TM_MOE, F), lambda j, i, be, nu: (jnp.minimum(i, nu[0] - 1), 0)),
            pl.BlockSpec((None, None, F, TN), lambda j, i, be, nu: (moe_layer, be[i], 0, j)),
        ],
        out_specs=pl.BlockSpec((TM_MOE, TN), lambda j, i, be, nu: (i, j)),
        scratch_shapes=[pltpu.VMEM((F, TN), BF16)],
    )
    return pl.pallas_call(
        _moe_down_kernel,
        grid_spec=grid_spec,
        out_shape=jax.ShapeDtypeStruct((P, D_MODEL), F32),
        compiler_params=_params(("arbitrary", "arbitrary")),
        name="moe_down",
    )(blk_exp, nused, act, w_down)


def _combine_kernel(d1_ref, d2_ref, y_hbm, info_ref, h_ref, g_ref, b_ref, ho_ref, hbo_ref, y1_ref, y2_ref, sem):
    rows = y1_ref.shape[0]

    def issue(r, c):
        _row_copy(y_hbm, d1_ref[0, 0, r], y1_ref, r, sem.at[0]).start()
        _row_copy(y_hbm, d2_ref[0, 0, r], y2_ref, r, sem.at[1]).start()
        return c

    lax.fori_loop(0, rows, issue, 0, unroll=8)

    def wait(r, c):
        _row_copy(y_hbm, 0, y1_ref, r, sem.at[0]).wait()
        _row_copy(y_hbm, 0, y2_ref, r, sem.at[1]).wait()
        return c

    lax.fori_loop(0, rows, wait, 0, unroll=8)
    info = info_ref[...]
    g1 = info[:, _L_G1:_L_G1 + 1]
    g2 = info[:, _L_G2:_L_G2 + 1]
    y = y1_ref[...] * g1 + y2_ref[...] * g2
    z = _layer_norm(ALPHA * h_ref[...] + y, g_ref[...], b_ref[...])
    ho_ref[...] = z
    hbo_ref[...] = z.astype(BF16)


def _combine(y, dest1, dest2, info, h, g, b, layer):
    T, D = h.shape
    row = pl.BlockSpec((TM, D), lambda i: (i, 0))
    vec = pl.BlockSpec((None, 1, D), lambda i: (layer, 0, 0))
    idx = pl.BlockSpec((1, 1, TM), lambda i: (i, 0, 0), memory_space=pltpu.SMEM)
    return pl.pallas_call(
        _combine_kernel,
        grid=(T // TM,),
        in_specs=[idx, idx, pl.BlockSpec(memory_space=pl.ANY), pl.BlockSpec((TM, LANES), lambda i: (i, 0)), row, vec, vec],
        out_specs=[row, row],
        out_shape=[jax.ShapeDtypeStruct((T, D), F32), jax.ShapeDtypeStruct((T, D), BF16)],
        scratch_shapes=[pltpu.VMEM((TM, D), F32), pltpu.VMEM((TM, D), F32), pltpu.SemaphoreType.DMA((2,))],
        compiler_params=_params(("arbitrary",)),
        name="moe_combine",
    )(dest1.reshape(T // TM, 1, TM), dest2.reshape(T // TM, 1, TM), y, info, h, g, b)


def _moe_ffn(h, w_router, w_up, w_down, moe_layer, ln_g, ln_b, layer):
    T, D = h.shape
    n_blk = (2 * T) // TM_MOE + N_EXPERTS
    w_router_pad = jnp.pad(w_router[moe_layer], ((0, 0), (0, LANES - N_EXPERTS)))
    info, cnt = _router(h, w_router_pad)
    counts = cnt[0, :N_EXPERTS].astype(jnp.int32)
    padded = (counts + TM_MOE - 1) // TM_MOE * TM_MOE
    pad_ends = jnp.cumsum(padded)
    pad_starts = pad_ends - padded
    e1 = info[:, _L_E1].astype(jnp.int32)
    e2 = info[:, _L_E2].astype(jnp.int32)
    dest1 = pad_starts[e1] + info[:, _L_R1].astype(jnp.int32)
    dest2 = pad_starts[e2] + info[:, _L_R2].astype(jnp.int32)
    tok = jnp.arange(T, dtype=jnp.int32)
    buf_tok = jnp.full((n_blk * TM_MOE,), T - 1, jnp.int32).at[dest1].set(tok).at[dest2].set(tok)
    nused = (pad_ends[-1] // TM_MOE).astype(jnp.int32).reshape(1)
    blk_start = jnp.minimum(jnp.arange(n_blk, dtype=jnp.int32), nused[0] - 1) * TM_MOE
    blk_exp = jnp.minimum(jnp.searchsorted(pad_ends, blk_start, side="right"), N_EXPERTS - 1).astype(jnp.int32)
    xs = _dispatch(h, buf_tok.reshape(n_blk, 1, TM_MOE), nused)
    act = _moe_up(xs, w_up, moe_layer, blk_exp, nused)
    y = _moe_down(act, w_down, moe_layer, blk_exp, nused)
    return _combine(y, dest1, dest2, info, h, ln_g, ln_b, layer)


def _rope_tables(S):
    t = jnp.arange(S, dtype=jnp.int32)
    inv_freq = ROPE_THETA ** (-jnp.arange(0, ROPE_AXIS_DIM, 2, dtype=F32) / ROPE_AXIS_DIM)
    ang_row = (t // GRID_W).astype(F32)[:, None] * inv_freq[None, :]
    ang_col = (t % GRID_W).astype(F32)[:, None] * inv_freq[None, :]
    cos = jnp.concatenate([jnp.cos(ang_row)] * 2 + [jnp.cos(ang_col)] * 2, axis=-1)
    sin = jnp.concatenate([-jnp.sin(ang_row), jnp.sin(ang_row), -jnp.sin(ang_col), jnp.sin(ang_col)], axis=-1)
    return cos, sin


def kernel(x, ln_in_g, ln_in_b, w_in, q_norm_g, k_norm_g, conv_w, conv_b, conv_ln_g, conv_ln_b, w_out, ln_mix_g, ln_mix_b, ffn_w_up, ffn_w_down, moe_router, moe_w_up, moe_w_down, ln_ffn_g, ln_ffn_b):
    B, S, D = x.shape
    T = B * S
    assert D == D_MODEL and T % TM_UP == 0 and S % TQ == 0
    cos_t, sin_t = _rope_tables(S)
    w_out_b = w_out.astype(BF16)
    ffn_w_down_b = ffn_w_down.astype(BF16)
    q_norm_g, k_norm_g, conv_b, conv_ln_g, conv_ln_b, ln_mix_g, ln_mix_b, ln_ffn_g, ln_ffn_b = (
        p.reshape(DEPTH, 1, -1)
        for p in (q_norm_g, k_norm_g, conv_b, conv_ln_g, conv_ln_b, ln_mix_g, ln_mix_b, ln_ffn_g, ln_ffn_b))
    h, hb = _ln_in(x.reshape(T, D), ln_in_g, ln_in_b)
    for layer in range(DEPTH):
        qkv = _qkv_proj(hb, w_in, layer, cos_t, sin_t, q_norm_g, k_norm_g, S)
        u = _glu_proj(hb, w_in, layer)
        attn = _attention(qkv.reshape(B, S, QKV_COLS)).reshape(T, ATTN_WIDTH)
        conv = _conv_module(u.reshape(B, S, CONV_CH), conv_w, conv_b, conv_ln_g, conv_ln_b, layer).reshape(T, CONV_CH)
        h, hb = _out_proj(attn, conv, w_out_b, h, ln_mix_g, ln_mix_b, layer)
        if layer % 2 == 0:
            act = _ffn_up(hb, ffn_w_up, layer // 2)
            h, hb = _ffn_down(act, ffn_w_down_b, h, ln_ffn_g, ln_ffn_b, layer // 2, layer)
        else:
            h, hb = _moe_ffn(h, moe_router, moe_w_up, moe_w_down, layer // 2, ln_ffn_g, ln_ffn_b, layer)
    return h.reshape(B, S, D)
```

```python
import functools

import jax
import jax.numpy as jnp
from jax import lax
from jax.experimental import pallas as pl
from jax.experimental.pallas import tpu as pltpu

F32 = jnp.float32
BF16 = jnp.bfloat16

D_MODEL = 2048
DEPTH = 4
GRID_W = 64
ATTN_WIDTH = 1024
CONV_CH = 1024
HEAD_DIM = 128
N_Q_HEADS = 8
N_KV_HEADS = 2
GQA_GROUP = 4
KV_WIDTH = 256
QKV_COLS = ATTN_WIDTH + 2 * KV_WIDTH
ROPE_THETA = 10000.0
ROPE_AXIS_DIM = 64
CONV_K = 31
CONV_HALO = 16
FF_DIM = 5632
N_EXPERTS = 8
ALPHA = (2.0 * DEPTH) ** 0.25
LN_EPS = 1e-5
RMS_EPS = 1e-6
LOG2_E = 1.4426950408889634

V7X_VMEM_BYTES = 64 * 1024 * 1024
VMEM_LIMIT = V7X_VMEM_BYTES - 8 * 1024 * 1024
LANES = 128
SUBLANES = 8

TM = 512
TM_UP = 1024
TN = 512
TQ = 512
TK = 1024
TS_CONV = 256
RC_CONV = 128
TM_MOE = 512
TK_DOWN = 1408


def _params(sem):
    return pltpu.CompilerParams(dimension_semantics=sem, vmem_limit_bytes=VMEM_LIMIT)


def _layer_norm(z, g, b):
    mu = jnp.mean(z, axis=-1, keepdims=True)
    zc = z - mu
    var = jnp.mean(zc * zc, axis=-1, keepdims=True)
    return zc * lax.rsqrt(var + LN_EPS) * g + b


def _ln_in_kernel(x_ref, g_ref, b_ref, h_ref, hb_ref):
    y = _layer_norm(x_ref[...], g_ref[...], b_ref[...])
    h_ref[...] = y
    hb_ref[...] = y.astype(BF16)


def _ln_in(x2, g, b):
    T, D = x2.shape
    row = pl.BlockSpec((TM, D), lambda i: (i, 0))
    vec = pl.BlockSpec((1, D), lambda i: (0, 0))
    return pl.pallas_call(
        _ln_in_kernel,
        grid=(T // TM,),
        in_specs=[row, vec, vec],
        out_specs=[row, row],
        out_shape=[jax.ShapeDtypeStruct((T, D), F32), jax.ShapeDtypeStruct((T, D), BF16)],
        compiler_params=_params(("parallel",)),
        name="ln_in",
    )(x2, g.reshape(1, D), b.reshape(1, D))


def _qkv_kernel(a_ref, w_ref, cos_ref, sin_ref, qg_ref, kg_ref, o_ref, wb_ref):
    j = pl.program_id(0)
    i = pl.program_id(1)

    @pl.when(i == 0)
    def _():
        wb_ref[...] = w_ref[...].astype(BF16)

    acc = jnp.dot(a_ref[...], wb_ref[...], preferred_element_type=F32)
    cos = cos_ref[...]
    sin = sin_ref[...]
    lane = lax.broadcasted_iota(jnp.int32, (TM, HEAD_DIM), 1)
    first_half = (lane & (ROPE_AXIS_DIM // 2)) == 0

    def norm_rope(xc, g, scale):
        ms = jnp.mean(xc * xc, axis=-1, keepdims=True)
        xn = xc * lax.rsqrt(ms + RMS_EPS) * g
        sw = jnp.where(first_half, pltpu.roll(xn, HEAD_DIM - 32, 1), pltpu.roll(xn, 32, 1))
        return (xn * cos + sw * sin) * scale

    n_chunks = TN // HEAD_DIM
    q_scale = HEAD_DIM ** -0.5 * LOG2_E

    @pl.when(j < ATTN_WIDTH // TN)
    def _():
        for c in range(n_chunks):
            cs = slice(c * HEAD_DIM, (c + 1) * HEAD_DIM)
            o_ref[:, cs] = norm_rope(acc[:, cs], qg_ref[...], q_scale).astype(BF16)

    @pl.when(j >= ATTN_WIDTH // TN)
    def _():
        for c in range(n_chunks):
            cs = slice(c * HEAD_DIM, (c + 1) * HEAD_DIM)
            if c < N_KV_HEADS:
                o_ref[:, cs] = norm_rope(acc[:, cs], kg_ref[...], 1.0).astype(BF16)
            else:
                o_ref[:, cs] = acc[:, cs].astype(BF16)


def _qkv_proj(hb, w_in, layer, cos_t, sin_t, q_g, k_g, S):
    T, D = hb.shape
    assert TN == 2 * KV_WIDTH and ATTN_WIDTH % TN == 0 and S % TM == 0
    s_tiles = S // TM
    return pl.pallas_call(
        _qkv_kernel,
        grid=(QKV_COLS // TN, T // TM),
        in_specs=[
            pl.BlockSpec((TM, D), lambda j, i: (i, 0)),
            pl.BlockSpec((None, D, TN), lambda j, i: (layer, 0, j)),
            pl.BlockSpec((TM, HEAD_DIM), lambda j, i: (i % s_tiles, 0)),
            pl.BlockSpec((TM, HEAD_DIM), lambda j, i: (i % s_tiles, 0)),
            pl.BlockSpec((None, 1, HEAD_DIM), lambda j, i: (layer, 0, 0)),
            pl.BlockSpec((None, 1, HEAD_DIM), lambda j, i: (layer, 0, 0)),
        ],
        out_specs=pl.BlockSpec((TM, TN), lambda j, i: (i, j)),
        out_shape=jax.ShapeDtypeStruct((T, QKV_COLS), BF16),
        scratch_shapes=[pltpu.VMEM((D, TN), BF16)],
        compiler_params=_params(("arbitrary", "arbitrary")),
        name="qkv_proj",
    )(hb, w_in, cos_t, sin_t, q_g, k_g)


def _glu_kernel(a_ref, wa_ref, wg_ref, o_ref, wab_ref, wgb_ref):
    i = pl.program_id(1)

    @pl.when(i == 0)
    def _():
        wab_ref[...] = wa_ref[...].astype(BF16)
        wgb_ref[...] = wg_ref[...].astype(BF16)

    a = a_ref[...]
    val = jnp.dot(a, wab_ref[...], preferred_element_type=F32)
    gate = jnp.dot(a, wgb_ref[...], preferred_element_type=F32)
    o_ref[...] = val * jax.nn.sigmoid(gate)


def _glu_proj(hb, w_in, layer):
    T, D = hb.shape
    val0 = QKV_COLS // TN
    gate0 = (QKV_COLS + CONV_CH) // TN
    return pl.pallas_call(
        _glu_kernel,
        grid=(CONV_CH // TN, T // TM),
        in_specs=[
            pl.BlockSpec((TM, D), lambda j, i: (i, 0)),
            pl.BlockSpec((None, D, TN), lambda j, i: (layer, 0, val0 + j)),
            pl.BlockSpec((None, D, TN), lambda j, i: (layer, 0, gate0 + j)),
        ],
        out_specs=pl.BlockSpec((TM, TN), lambda j, i: (i, j)),
        out_shape=jax.ShapeDtypeStruct((T, CONV_CH), F32),
        scratch_shapes=[pltpu.VMEM((D, TN), BF16), pltpu.VMEM((D, TN), BF16)],
        compiler_params=_params(("arbitrary", "arbitrary")),
        name="glu_proj",
    )(hb, w_in, w_in)


def _conv_kernel(cur_ref, prev_ref, next_ref, w_ref, cb_ref, g_ref, b_ref, o_ref, ext_ref, y_ref):
    s = pl.program_id(1)
    ts = cur_ref.shape[1]
    has_prev = (s > 0).astype(F32)
    has_next = (s < pl.num_programs(1) - 1).astype(F32)
    ext_ref[0:CONV_HALO, :] = prev_ref[0] * has_prev
    ext_ref[CONV_HALO:CONV_HALO + ts, :] = cur_ref[0]
    ext_ref[CONV_HALO + ts:, :] = next_ref[0] * has_next
    off = CONV_HALO - CONV_K // 2
    span = (off + CONV_K - 1) // SUBLANES * SUBLANES
    for c in range(CONV_CH // LANES):
        cs = slice(c * LANES, (c + 1) * LANES)
        for r in range(ts // RC_CONV):
            r0 = r * RC_CONV
            acc = jnp.broadcast_to(cb_ref[:, cs], (RC_CONV, LANES))
            for phase in range(SUBLANES):
                taps = [k for k in range(CONV_K) if (off + k) % SUBLANES == phase]
                shifted = ext_ref[r0 + phase:r0 + phase + RC_CONV + span, cs]
                for k in taps:
                    a0 = off + k - phase
                    acc = acc + shifted[a0:a0 + RC_CONV, :] * w_ref[k:k + 1, cs]
            y_ref[r0:r0 + RC_CONV, cs] = acc
    y = _layer_norm(y_ref[...], g_ref[...], b_ref[...])
    o_ref[0] = (y * jax.nn.sigmoid(y)).astype(BF16)


def _conv_module(u3, conv_w, conv_b, conv_g, conv_bt, layer):
    B, S, C = u3.shape
    ts = TS_CONV
    assert S % ts == 0 and ts % CONV_HALO == 0 and ts % RC_CONV == 0
    halo_per_tile = ts // CONV_HALO
    n_halo = S // CONV_HALO
    vec = pl.BlockSpec((None, 1, C), lambda b, s: (layer, 0, 0))
    return pl.pallas_call(
        _conv_kernel,
        grid=(B, S // ts),
        in_specs=[
            pl.BlockSpec((1, ts, C), lambda b, s: (b, s, 0)),
            pl.BlockSpec((1, CONV_HALO, C), lambda b, s: (b, jnp.maximum(s * halo_per_tile - 1, 0), 0)),
            pl.BlockSpec((1, CONV_HALO, C), lambda b, s: (b, jnp.minimum((s + 1) * halo_per_tile, n_halo - 1), 0)),
            pl.BlockSpec((None, CONV_K, C), lambda b, s: (layer, 0, 0)),
            vec, vec, vec,
        ],
        out_specs=pl.BlockSpec((1, ts, C), lambda b, s: (b, s, 0)),
        out_shape=jax.ShapeDtypeStruct((B, S, C), BF16),
        scratch_shapes=[pltpu.VMEM((ts + 2 * CONV_HALO, C), F32), pltpu.VMEM((ts, C), F32)],
        compiler_params=_params(("parallel", "parallel")),
        name="conv_module",
    )(u3, u3, u3, conv_w, conv_b, conv_g, conv_bt)


def _lane_tree(op, x):
    parts = [x[:, j * LANES:(j + 1) * LANES] for j in range(x.shape[1] // LANES)]
    while len(parts) > 1:
        parts = [op(parts[i], parts[i + 1]) for i in range(0, len(parts) - 1, 2)] + (parts[-1:] if len(parts) % 2 else [])
    return parts[0]


def _attn_kernel(q_ref, k_ref, v_ref, o_ref, sa_ref, sb_ref, ma_ref, mb_ref):
    n = pl.program_id(0)
    n_chunks = k_ref.shape[1] // TK

    @pl.when(n == 0)
    def _():
        sb_ref[...] = jnp.zeros(sb_ref.shape, F32)
        mb_ref[...] = jnp.zeros(mb_ref.shape, F32)

    def step(s_cur, m_cur, s_prev, m_prev):
        q = q_ref[0]
        m = m_prev[:, 0:1]
        mp = jnp.full((TQ, LANES), -jnp.inf, F32)
        lp = jnp.zeros((TQ, LANES), F32)
        acc = jnp.zeros((TQ, HEAD_DIM), F32)
        for c in range(n_chunks):
            ks = slice(c * TK, (c + 1) * TK)
            s_c = lax.dot_general(q, k_ref[0, ks, :], (((1,), (1,)), ((), ())), preferred_element_type=F32)
            s_cur[:, ks] = s_c
            mp = jnp.maximum(mp, _lane_tree(jnp.maximum, s_c))
            p = jnp.exp2(s_prev[:, ks] - m)
            lp = lp + _lane_tree(jnp.add, p)
            acc = acc + jnp.dot(p.astype(BF16), v_ref[0, ks, :], preferred_element_type=F32)
        m_cur[...] = jnp.broadcast_to(jnp.max(mp, axis=-1, keepdims=True), (TQ, LANES))
        l = jnp.sum(lp, axis=-1, keepdims=True)
        o_ref[0] = (acc / l).astype(BF16)

    @pl.when(n % 2 == 0)
    def _():
        step(sa_ref, ma_ref, sb_ref, mb_ref)

    @pl.when(n % 2 == 1)
    def _():
        step(sb_ref, mb_ref, sa_ref, ma_ref)


def _attention(qkv3):
    B, S, _ = qkv3.shape
    assert S % TK == 0
    k0 = ATTN_WIDTH // HEAD_DIM
    v0 = (ATTN_WIDTH + KV_WIDTH) // HEAD_DIM
    nq = S // TQ
    n_tiles = B * N_Q_HEADS * nq

    def tile(t):
        return t // (nq * N_Q_HEADS), (t // nq) % N_Q_HEADS, t % nq

    def q_map(n):
        b, h, qi = tile(jnp.minimum(n, n_tiles - 1))
        return b, qi, h

    def k_map(n):
        b, h, _ = tile(jnp.minimum(n, n_tiles - 1))
        return b, 0, k0 + h // GQA_GROUP

    def v_map(n):
        b, h, _ = tile(jnp.maximum(n - 1, 0))
        return b, 0, v0 + h // GQA_GROUP

    def o_map(n):
        b, h, qi = tile(jnp.maximum(n - 1, 0))
        return b, qi, h

    return pl.pallas_call(
        _attn_kernel,
        grid=(n_tiles + 1,),
        in_specs=[
            pl.BlockSpec((1, TQ, HEAD_DIM), q_map),
            pl.BlockSpec((1, S, HEAD_DIM), k_map),
            pl.BlockSpec((1, S, HEAD_DIM), v_map),
        ],
        out_specs=pl.BlockSpec((1, TQ, HEAD_DIM), o_map),
        out_shape=jax.ShapeDtypeStruct((B, S, ATTN_WIDTH), BF16),
        scratch_shapes=[pltpu.VMEM((TQ, S), F32), pltpu.VMEM((TQ, S), F32),
                        pltpu.VMEM((TQ, LANES), F32), pltpu.VMEM((TQ, LANES), F32)],
        compiler_params=_params(("arbitrary",)),
        name="attention",
    )(qkv3, qkv3, qkv3)


def _out_proj_kernel(attn_ref, conv_ref, w_ref, h_ref, g_ref, b_ref, ho_ref, hbo_ref):
    y = jnp.dot(attn_ref[...], w_ref[0:ATTN_WIDTH, :], preferred_element_type=F32)
    y = y + jnp.dot(conv_ref[...], w_ref[ATTN_WIDTH:, :], preferred_element_type=F32)
    z = _layer_norm(ALPHA * h_ref[...] + y, g_ref[...], b_ref[...])
    ho_ref[...] = z
    hbo_ref[...] = z.astype(BF16)


def _out_proj(attn, conv, w_out_b, h, g, b, layer):
    T, D = h.shape
    row = pl.BlockSpec((TM, D), lambda i: (i, 0))
    half = pl.BlockSpec((TM, ATTN_WIDTH), lambda i: (i, 0))
    vec = pl.BlockSpec((None, 1, D), lambda i: (layer, 0, 0))
    return pl.pallas_call(
        _out_proj_kernel,
        grid=(T // TM,),
        in_specs=[half, half, pl.BlockSpec((None, D, D), lambda i: (layer, 0, 0)), row, vec, vec],
        out_specs=[row, row],
        out_shape=[jax.ShapeDtypeStruct((T, D), F32), jax.ShapeDtypeStruct((T, D), BF16)],
        compiler_params=_params(("parallel",)),
        name="out_proj",
    )(attn, conv, w_out_b, h, g, b)


def _ffn_up_kernel(a_ref, wg_ref, wu_ref, o_ref, wgb_ref, wub_ref):
    i = pl.program_id(1)

    @pl.when(i == 0)
    def _():
        wgb_ref[...] = wg_ref[...].astype(BF16)
        wub_ref[...] = wu_ref[...].astype(BF16)

    a = a_ref[...]
    g = jnp.dot(a, wgb_ref[...], preferred_element_type=F32)
    u = jnp.dot(a, wub_ref[...], preferred_element_type=F32)
    o_ref[...] = (g * jax.nn.sigmoid(g) * u).astype(BF16)


def _ffn_up(hb, w_up, layer):
    T, D = hb.shape
    nf = FF_DIM // TN
    return pl.pallas_call(
        _ffn_up_kernel,
        grid=(nf, T // TM_UP),
        in_specs=[
            pl.BlockSpec((TM_UP, D), lambda f, i: (i, 0)),
            pl.BlockSpec((None, D, TN), lambda f, i: (layer, 0, f)),
            pl.BlockSpec((None, D, TN), lambda f, i: (layer, 0, nf + f)),
        ],
        out_specs=pl.BlockSpec((TM_UP, TN), lambda f, i: (i, f)),
        out_shape=jax.ShapeDtypeStruct((T, FF_DIM), BF16),
        scratch_shapes=[pltpu.VMEM((D, TN), BF16), pltpu.VMEM((D, TN), BF16)],
        compiler_params=_params(("arbitrary", "arbitrary")),
        name="ffn_up",
    )(hb, w_up, w_up)


def _ffn_down_kernel(a_ref, w_ref, h_ref, g_ref, b_ref, ho_ref, hbo_ref, acc_ref):
    k = pl.program_id(1)
    part = jnp.dot(a_ref[...], w_ref[...], preferred_element_type=F32)

    @pl.when(k == 0)
    def _():
        acc_ref[...] = part

    @pl.when(k > 0)
    def _():
        acc_ref[...] += part

    @pl.when(k == pl.num_programs(1) - 1)
    def _():
        z = _layer_norm(ALPHA * h_ref[...] + acc_ref[...], g_ref[...], b_ref[...])
        ho_ref[...] = z
        hbo_ref[...] = z.astype(BF16)


def _ffn_down(act, w_down_b, h, g, b, dense_layer, layer):
    T, D = h.shape
    row = pl.BlockSpec((TM, D), lambda i, k: (i, 0))
    vec = pl.BlockSpec((None, 1, D), lambda i, k: (layer, 0, 0))
    return pl.pallas_call(
        _ffn_down_kernel,
        grid=(T // TM, FF_DIM // TK_DOWN),
        in_specs=[
            pl.BlockSpec((TM, TK_DOWN), lambda i, k: (i, k)),
            pl.BlockSpec((None, TK_DOWN, D), lambda i, k: (dense_layer, k, 0)),
            row, vec, vec,
        ],
        out_specs=[row, row],
        out_shape=[jax.ShapeDtypeStruct((T, D), F32), jax.ShapeDtypeStruct((T, D), BF16)],
        scratch_shapes=[pltpu.VMEM((TM, D), F32)],
        compiler_params=_params(("parallel", "arbitrary")),
        name="ffn_down",
    )(act, w_down_b, h, g, b)


_L_E1, _L_E2, _L_R1, _L_R2, _L_G1, _L_G2 = range(6)


def _router_kernel(h_ref, wr_ref, info_ref, cnt_ref, carry_ref):
    i = pl.program_id(0)

    @pl.when(i == 0)
    def _():
        carry_ref[...] = jnp.zeros_like(carry_ref)

    logits = jnp.dot(h_ref[...], wr_ref[...], preferred_element_type=F32, precision=lax.Precision.HIGHEST)
    lane = lax.broadcasted_iota(jnp.int32, logits.shape, 1).astype(F32)
    neg = jnp.float32(-jnp.inf)
    lg = jnp.where(lane < N_EXPERTS, logits, neg)
    m1 = jnp.max(lg, axis=-1, keepdims=True)
    i1 = jnp.min(jnp.where(lg == m1, lane, float(LANES)), axis=-1, keepdims=True)
    oh1 = lane == i1
    lg2 = jnp.where(oh1, neg, lg)
    m2 = jnp.max(lg2, axis=-1, keepdims=True)
    i2 = jnp.min(jnp.where(lg2 == m2, lane, float(LANES)), axis=-1, keepdims=True)
    oh2 = lane == i2
    e = jnp.exp(m2 - m1)
    g1 = 1.0 / (1.0 + e)
    g2 = e * g1
    cnt = oh1.astype(F32) + oh2.astype(F32)
    tm = cnt.shape[0]
    row = lax.broadcasted_iota(jnp.int32, (tm, tm), 0)
    col = lax.broadcasted_iota(jnp.int32, (tm, tm), 1)
    strict_lower = jnp.where(col < row, 1.0, 0.0).astype(BF16)
    before = jnp.dot(strict_lower, cnt.astype(BF16), preferred_element_type=F32) + carry_ref[...]
    r1 = jnp.sum(jnp.where(oh1, before, 0.0), axis=-1, keepdims=True)
    r2 = jnp.sum(jnp.where(oh2, before, 0.0), axis=-1, keepdims=True)
    carry_ref[...] += jnp.sum(cnt, axis=0, keepdims=True)
    cnt_ref[...] = jnp.broadcast_to(carry_ref[...], cnt_ref.shape)
    info = jnp.zeros_like(logits)
    for idx, val in ((_L_E1, i1), (_L_E2, i2), (_L_R1, r1), (_L_R2, r2), (_L_G1, g1), (_L_G2, g2)):
        info = jnp.where(lane == idx, val, info)
    info_ref[...] = info


def _router(h, w_router_pad):
    T, D = h.shape
    return pl.pallas_call(
        _router_kernel,
        grid=(T // TM,),
        in_specs=[pl.BlockSpec((TM, D), lambda i: (i, 0)), pl.BlockSpec((D, LANES), lambda i: (0, 0))],
        out_specs=[pl.BlockSpec((TM, LANES), lambda i: (i, 0)), pl.BlockSpec((8, LANES), lambda i: (0, 0))],
        out_shape=[jax.ShapeDtypeStruct((T, LANES), F32), jax.ShapeDtypeStruct((8, LANES), F32)],
        scratch_shapes=[pltpu.VMEM((1, LANES), F32)],
        compiler_params=_params(("arbitrary",)),
        name="moe_router",
    )(h, w_router_pad)


def _row_copy(src_hbm, src_row, dst, dst_row, sem):
    return pltpu.make_async_copy(src_hbm.at[pl.ds(src_row, 1), :], dst.at[pl.ds(dst_row, 1), :], sem)


def _dispatch_kernel(nused_ref, tok_ref, h_hbm, o_ref, buf_ref, sem):
    i = pl.program_id(0)
    rows = buf_ref.shape[0]

    @pl.when(i < nused_ref[0])
    def _():
        def issue(r, c):
            _row_copy(h_hbm, tok_ref[0, 0, r], buf_ref, r, sem).start()
            return c

        lax.fori_loop(0, rows, issue, 0, unroll=8)

        def wait(r, c):
            _row_copy(h_hbm, 0, buf_ref, r, sem).wait()
            return c

        lax.fori_loop(0, rows, wait, 0, unroll=8)
        o_ref[...] = buf_ref[...].astype(BF16)

    @pl.when(i >= nused_ref[0])
    def _():
        o_ref[...] = jnp.zeros_like(o_ref)


def _dispatch(h, buf_tok3, nused):
    T, D = h.shape
    n_blk = buf_tok3.shape[0]
    grid_spec = pltpu.PrefetchScalarGridSpec(
        num_scalar_prefetch=1,
        grid=(n_blk,),
        in_specs=[
            pl.BlockSpec((1, 1, TM_MOE), lambda i, nu: (i, 0, 0), memory_space=pltpu.SMEM),
            pl.BlockSpec(memory_space=pl.ANY),
        ],
        out_specs=pl.BlockSpec((TM_MOE, D), lambda i, nu: (i, 0)),
        scratch_shapes=[pltpu.VMEM((TM_MOE, D), F32), pltpu.SemaphoreType.DMA(())],
    )
    return pl.pallas_call(
        _dispatch_kernel,
        grid_spec=grid_spec,
        out_shape=jax.ShapeDtypeStruct((n_blk * TM_MOE, D), BF16),
        compiler_params=_params(("arbitrary",)),
        name="moe_dispatch",
    )(nused, buf_tok3, h)


def _expert_changed(be_ref, i):
    return jnp.logical_or(i == 0, be_ref[i] != be_ref[jnp.maximum(i - 1, 0)])


def _moe_up_kernel(be_ref, nused_ref, x_ref, wg_ref, wu_ref, o_ref, wgb_ref, wub_ref):
    i = pl.program_id(1)

    @pl.when(_expert_changed(be_ref, i))
    def _():
        wgb_ref[...] = wg_ref[...].astype(BF16)
        wub_ref[...] = wu_ref[...].astype(BF16)

    @pl.when(i < nused_ref[0])
    def _():
        x = x_ref[...]
        g = jnp.dot(x, wgb_ref[...], preferred_element_type=F32)
        u = jnp.dot(x, wub_ref[...], preferred_element_type=F32)
        o_ref[...] = (g * jax.nn.sigmoid(g) * u).astype(BF16)

    @pl.when(i >= nused_ref[0])
    def _():
        o_ref[...] = jnp.zeros_like(o_ref)


def _moe_up(xs, w_up, moe_layer, blk_exp, nused):
    P, D = xs.shape
    nf = FF_DIM // TN
    n_blk = P // TM_MOE
    grid_spec = pltpu.PrefetchScalarGridSpec(
        num_scalar_prefetch=2,
        grid=(nf, n_blk),
        in_specs=[
            pl.BlockSpec((TM_MOE, D), lambda f, i, be, nu: (jnp.minimum(i, nu[0] - 1), 0)),
            pl.BlockSpec((None, None, D, TN), lambda f, i, be, nu: (moe_layer, be[i], 0, f)),
            pl.BlockSpec((None, None, D, TN), lambda f, i, be, nu: (moe_layer, be[i], 0, nf + f)),
        ],
        out_specs=pl.BlockSpec((TM_MOE, TN), lambda f, i, be, nu: (i, f)),
        scratch_shapes=[pltpu.VMEM((D, TN), BF16), pltpu.VMEM((D, TN), BF16)],
    )
    return pl.pallas_call(
        _moe_up_kernel,
        grid_spec=grid_spec,
        out_shape=jax.ShapeDtypeStruct((P, FF_DIM), BF16),
        compiler_params=_params(("arbitrary", "arbitrary")),
        name="moe_up",
    )(blk_exp, nused, xs, w_up, w_up)


def _moe_down_kernel(be_ref, nused_ref, a_ref, w_ref, o_ref, wb_ref):
    i = pl.program_id(1)

    @pl.when(_expert_changed(be_ref, i))
    def _():
        wb_ref[...] = w_ref[...].astype(BF16)

    @pl.when(i < nused_ref[0])
    def _():
        o_ref[...] = jnp.dot(a_ref[...], wb_ref[...], preferred_element_type=F32)

    @pl.when(i >= nused_ref[0])
    def _():
        o_ref[...] = jnp.zeros_like(o_ref)


def _moe_down(act, w_down, moe_layer, blk_exp, nused):
    P, F = act.shape
    n_blk = P // TM_MOE
    grid_spec = pltpu.PrefetchScalarGridSpec(
        num_scalar_prefetch=2,
        grid=(D_MODEL // TN, n_blk),
        in_specs=[
            pl.BlockSpec((TM_MOE, F), lambda j, i, be, nu: (jnp.minimum(i, nu[0] - 1), 0)),
            pl.BlockSpec((None, None, F, TN), lambda j, i, be, nu: (moe_layer, be[i], 0, j)),
        ],
        out_specs=pl.BlockSpec((TM_MOE, TN), lambda j, i, be, nu: (i, j)),
        scratch_shapes=[pltpu.VMEM((F, TN), BF16)],
    )
    return pl.pallas_call(
        _moe_down_kernel,
        grid_spec=grid_spec,
        out_shape=jax.ShapeDtypeStruct((P, D_MODEL), F32),
        compiler_params=_params(("arbitrary", "arbitrary")),
        name="moe_down",
    )(blk_exp, nused, act, w_down)


def _combine_kernel(d1_ref, d2_ref, y_hbm, info_ref, h_ref, g_ref, b_ref, ho_ref, hbo_ref, y1_ref, y2_ref, sem):
    rows = y1_ref.shape[0]

    def issue(r, c):
        _row_copy(y_hbm, d1_ref[0, 0, r], y1_ref, r, sem.at[0]).start()
        _row_copy(y_hbm, d2_ref[0, 0, r], y2_ref, r, sem.at[1]).start()
        return c

    lax.fori_loop(0, rows, issue, 0, unroll=8)

    def wait(r, c):
        _row_copy(y_hbm, 0, y1_ref, r, sem.at[0]).wait()
        _row_copy(y_hbm, 0, y2_ref, r, sem.at[1]).wait()
        return c

    lax.fori_loop(0, rows, wait, 0, unroll=8)
    info = info_ref[...]
    g1 = info[:, _L_G1:_L_G1 + 1]
    g2 = info[:, _L_G2:_L_G2 + 1]
    y = y1_ref[...] * g1 + y2_ref[...] * g2
    z = _layer_norm(ALPHA * h_ref[...] + y, g_ref[...], b_ref[...])
    ho_ref[...] = z
    hbo_ref[...] = z.astype(BF16)


def _combine(y, dest1, dest2, info, h, g, b, layer):
    T, D = h.shape
    row = pl.BlockSpec((TM, D), lambda i: (i, 0))
    vec = pl.BlockSpec((None, 1, D), lambda i: (layer, 0, 0))
    idx = pl.BlockSpec((1, 1, TM), lambda i: (i, 0, 0), memory_space=pltpu.SMEM)
    return pl.pallas_call(
        _combine_kernel,
        grid=(T // TM,),
        in_specs=[idx, idx, pl.BlockSpec(memory_space=pl.ANY), pl.BlockSpec((TM, LANES), lambda i: (i, 0)), row, vec, vec],
        out_specs=[row, row],
        out_shape=[jax.ShapeDtypeStruct((T, D), F32), jax.ShapeDtypeStruct((T, D), BF16)],
        scratch_shapes=[pltpu.VMEM((TM, D), F32), pltpu.VMEM((TM, D), F32), pltpu.SemaphoreType.DMA((2,))],
        compiler_params=_params(("arbitrary",)),
        name="moe_combine",
    )(dest1.reshape(T // TM, 1, TM), dest2.reshape(T // TM, 1, TM), y, info, h, g, b)


def _moe_ffn(h, w_router, w_up, w_down, moe_layer, ln_g, ln_b, layer):
    T, D = h.shape
    n_blk = (2 * T) // TM_MOE + N_EXPERTS
    w_router_pad = jnp.pad(w_router[moe_layer], ((0, 0), (0, LANES - N_EXPERTS)))
    info, cnt = _router(h, w_router_pad)
    counts = cnt[0, :N_EXPERTS].astype(jnp.int32)
    padded = (counts + TM_MOE - 1) // TM_MOE * TM_MOE
    pad_ends = jnp.cumsum(padded)
    pad_starts = pad_ends - padded
    e1 = info[:, _L_E1].astype(jnp.int32)
    e2 = info[:, _L_E2].astype(jnp.int32)
    dest1 = pad_starts[e1] + info[:, _L_R1].astype(jnp.int32)
    dest2 = pad_starts[e2] + info[:, _L_R2].astype(jnp.int32)
    tok = jnp.arange(T, dtype=jnp.int32)
    buf_tok = jnp.full((n_blk * TM_MOE,), T - 1, jnp.int32).at[jnp.concatenate([dest1, dest2])].set(
        jnp.concatenate([tok, tok]), unique_indices=True, mode="promise_in_bounds")
    nused = (pad_ends[-1] // TM_MOE).astype(jnp.int32).reshape(1)
    blk_start = jnp.minimum(jnp.arange(n_blk, dtype=jnp.int32), nused[0] - 1) * TM_MOE
    blk_exp = jnp.minimum(jnp.searchsorted(pad_ends, blk_start, side="right"), N_EXPERTS - 1).astype(jnp.int32)
    xs = _dispatch(h, buf_tok.reshape(n_blk, 1, TM_MOE), nused)
    act = _moe_up(xs, w_up, moe_layer, blk_exp, nused)
    y = _moe_down(act, w_down, moe_layer, blk_exp, nused)
    return _combine(y, dest1, dest2, info, h, ln_g, ln_b, layer)


def _rope_tables(S):
    t = jnp.arange(S, dtype=jnp.int32)
    inv_freq = ROPE_THETA ** (-jnp.arange(0, ROPE_AXIS_DIM, 2, dtype=F32) / ROPE_AXIS_DIM)
    ang_row = (t // GRID_W).astype(F32)[:, None] * inv_freq[None, :]
    ang_col = (t % GRID_W).astype(F32)[:, None] * inv_freq[None, :]
    cos = jnp.concatenate([jnp.cos(ang_row)] * 2 + [jnp.cos(ang_col)] * 2, axis=-1)
    sin = jnp.concatenate([-jnp.sin(ang_row), jnp.sin(ang_row), -jnp.sin(ang_col), jnp.sin(ang_col)], axis=-1)
    return cos, sin


def kernel(x, ln_in_g, ln_in_b, w_in, q_norm_g, k_norm_g, conv_w, conv_b, conv_ln_g, conv_ln_b, w_out, ln_mix_g, ln_mix_b, ffn_w_up, ffn_w_down, moe_router, moe_w_up, moe_w_down, ln_ffn_g, ln_ffn_b):
    B, S, D = x.shape
    T = B * S
    assert D == D_MODEL and T % TM_UP == 0 and S % TQ == 0
    cos_t, sin_t = _rope_tables(S)
    w_out_b = w_out.astype(BF16)
    ffn_w_down_b = ffn_w_down.astype(BF16)
    q_norm_g, k_norm_g, conv_b, conv_ln_g, conv_ln_b, ln_mix_g, ln_mix_b, ln_ffn_g, ln_ffn_b = (
        p.reshape(DEPTH, 1, -1)
        for p in (q_norm_g, k_norm_g, conv_b, conv_ln_g, conv_ln_b, ln_mix_g, ln_mix_b, ln_ffn_g, ln_ffn_b))
    h, hb = _ln_in(x.reshape(T, D), ln_in_g, ln_in_b)
    for layer in range(DEPTH):
        qkv = _qkv_proj(hb, w_in, layer, cos_t, sin_t, q_norm_g, k_norm_g, S)
        u = _glu_proj(hb, w_in, layer)
        attn = _attention(qkv.reshape(B, S, QKV_COLS)).reshape(T, ATTN_WIDTH)
        conv = _conv_module(u.reshape(B, S, CONV_CH), conv_w, conv_b, conv_ln_g, conv_ln_b, layer).reshape(T, CONV_CH)
        h, hb = _out_proj(attn, conv, w_out_b, h, ln_mix_g, ln_mix_b, layer)
        if layer % 2 == 0:
            act = _ffn_up(hb, ffn_w_up, layer // 2)
            h, hb = _ffn_down(act, ffn_w_down_b, h, ln_ffn_g, ln_ffn_b, layer // 2, layer)
        else:
            h, hb = _moe_ffn(h, moe_router, moe_w_up, moe_w_down, layer // 2, ln_ffn_g, ln_ffn_b, layer)
    return h.reshape(B, S, D)
```

```python
import functools

import jax
import jax.numpy as jnp
from jax import lax
from jax.experimental import pallas as pl
from jax.experimental.pallas import tpu as pltpu

F32 = jnp.float32
BF16 = jnp.bfloat16

D_MODEL = 2048
DEPTH = 4
GRID_W = 64
ATTN_WIDTH = 1024
CONV_CH = 1024
HEAD_DIM = 128
N_Q_HEADS = 8
N_KV_HEADS = 2
GQA_GROUP = 4
KV_WIDTH = 256
QKV_COLS = ATTN_WIDTH + 2 * KV_WIDTH
ROPE_THETA = 10000.0
ROPE_AXIS_DIM = 64
CONV_K = 31
CONV_HALO = 16
FF_DIM = 5632
N_EXPERTS = 8
ALPHA = (2.0 * DEPTH) ** 0.25
LN_EPS = 1e-5
RMS_EPS = 1e-6
LOG2_E = 1.4426950408889634

V7X_VMEM_BYTES = 64 * 1024 * 1024
VMEM_LIMIT = V7X_VMEM_BYTES - 8 * 1024 * 1024
LANES = 128
SUBLANES = 8

TM = 512
TM_UP = 1024
TN = 512
TQ = 512
TK = 1024
RB_ATTN = 32
TS_CONV = 256
RC_CONV = 128
TM_MOE = 512
TK_DOWN = 1408


def _params(sem, flags=None):
    return pltpu.CompilerParams(dimension_semantics=sem, vmem_limit_bytes=VMEM_LIMIT, flags=flags)


def _layer_norm(z, g, b):
    mu = jnp.mean(z, axis=-1, keepdims=True)
    zc = z - mu
    var = jnp.mean(zc * zc, axis=-1, keepdims=True)
    return zc * lax.rsqrt(var + LN_EPS) * g + b


def _ln_in_kernel(x_ref, g_ref, b_ref, h_ref, hb_ref):
    y = _layer_norm(x_ref[...], g_ref[...], b_ref[...])
    h_ref[...] = y
    hb_ref[...] = y.astype(BF16)


def _ln_in(x2, g, b):
    T, D = x2.shape
    row = pl.BlockSpec((TM, D), lambda i: (i, 0))
    vec = pl.BlockSpec((1, D), lambda i: (0, 0))
    return pl.pallas_call(
        _ln_in_kernel,
        grid=(T // TM,),
        in_specs=[row, vec, vec],
        out_specs=[row, row],
        out_shape=[jax.ShapeDtypeStruct((T, D), F32), jax.ShapeDtypeStruct((T, D), BF16)],
        compiler_params=_params(("parallel",)),
        name="ln_in",
    )(x2, g.reshape(1, D), b.reshape(1, D))


def _qkv_kernel(a_ref, w_ref, cos_ref, sin_ref, qg_ref, kg_ref, o_ref, wb_ref):
    j = pl.program_id(0)
    i = pl.program_id(1)

    @pl.when(i == 0)
    def _():
        wb_ref[...] = w_ref[...].astype(BF16)

    acc = jnp.dot(a_ref[...], wb_ref[...], preferred_element_type=F32)
    cos = cos_ref[...]
    sin = sin_ref[...]
    lane = lax.broadcasted_iota(jnp.int32, (TM, HEAD_DIM), 1)
    first_half = (lane & (ROPE_AXIS_DIM // 2)) == 0

    def norm_rope(xc, g, scale):
        ms = jnp.mean(xc * xc, axis=-1, keepdims=True)
        xn = xc * lax.rsqrt(ms + RMS_EPS) * g
        sw = jnp.where(first_half, pltpu.roll(xn, HEAD_DIM - 32, 1), pltpu.roll(xn, 32, 1))
        return (xn * cos + sw * sin) * scale

    n_chunks = TN // HEAD_DIM
    q_scale = HEAD_DIM ** -0.5 * LOG2_E

    @pl.when(j < ATTN_WIDTH // TN)
    def _():
        for c in range(n_chunks):
            cs = slice(c * HEAD_DIM, (c + 1) * HEAD_DIM)
            o_ref[:, cs] = norm_rope(acc[:, cs], qg_ref[...], q_scale).astype(BF16)

    @pl.when(j >= ATTN_WIDTH // TN)
    def _():
        for c in range(n_chunks):
            cs = slice(c * HEAD_DIM, (c + 1) * HEAD_DIM)
            if c < N_KV_HEADS:
                o_ref[:, cs] = norm_rope(acc[:, cs], kg_ref[...], 1.0).astype(BF16)
            else:
                o_ref[:, cs] = acc[:, cs].astype(BF16)


def _qkv_proj(hb, w_in, layer, cos_t, sin_t, q_g, k_g, S):
    T, D = hb.shape
    assert TN == 2 * KV_WIDTH and ATTN_WIDTH % TN == 0 and S % TM == 0
    s_tiles = S // TM
    return pl.pallas_call(
        _qkv_kernel,
        grid=(QKV_COLS // TN, T // TM),
        in_specs=[
            pl.BlockSpec((TM, D), lambda j, i: (i, 0)),
            pl.BlockSpec((None, D, TN), lambda j, i: (layer, 0, j)),
            pl.BlockSpec((TM, HEAD_DIM), lambda j, i: (i % s_tiles, 0)),
            pl.BlockSpec((TM, HEAD_DIM), lambda j, i: (i % s_tiles, 0)),
            pl.BlockSpec((None, 1, HEAD_DIM), lambda j, i: (layer, 0, 0)),
            pl.BlockSpec((None, 1, HEAD_DIM), lambda j, i: (layer, 0, 0)),
        ],
        out_specs=pl.BlockSpec((TM, TN), lambda j, i: (i, j)),
        out_shape=jax.ShapeDtypeStruct((T, QKV_COLS), BF16),
        scratch_shapes=[pltpu.VMEM((D, TN), BF16)],
        compiler_params=_params(("arbitrary", "arbitrary")),
        name="qkv_proj",
    )(hb, w_in, cos_t, sin_t, q_g, k_g)


def _glu_kernel(a_ref, wa_ref, wg_ref, o_ref, wab_ref, wgb_ref):
    i = pl.program_id(1)

    @pl.when(i == 0)
    def _():
        wab_ref[...] = wa_ref[...].astype(BF16)
        wgb_ref[...] = wg_ref[...].astype(BF16)

    a = a_ref[...]
    val = jnp.dot(a, wab_ref[...], preferred_element_type=F32)
    gate = jnp.dot(a, wgb_ref[...], preferred_element_type=F32)
    o_ref[...] = val * jax.nn.sigmoid(gate)


def _glu_proj(hb, w_in, layer):
    T, D = hb.shape
    val0 = QKV_COLS // TN
    gate0 = (QKV_COLS + CONV_CH) // TN
    return pl.pallas_call(
        _glu_kernel,
        grid=(CONV_CH // TN, T // TM),
        in_specs=[
            pl.BlockSpec((TM, D), lambda j, i: (i, 0)),
            pl.BlockSpec((None, D, TN), lambda j, i: (layer, 0, val0 + j)),
            pl.BlockSpec((None, D, TN), lambda j, i: (layer, 0, gate0 + j)),
        ],
        out_specs=pl.BlockSpec((TM, TN), lambda j, i: (i, j)),
        out_shape=jax.ShapeDtypeStruct((T, CONV_CH), F32),
        scratch_shapes=[pltpu.VMEM((D, TN), BF16), pltpu.VMEM((D, TN), BF16)],
        compiler_params=_params(("arbitrary", "arbitrary")),
        name="glu_proj",
    )(hb, w_in, w_in)


def _conv_kernel(cur_ref, prev_ref, next_ref, w_ref, cb_ref, g_ref, b_ref, o_ref, ext_ref, y_ref):
    s = pl.program_id(1)
    ts = cur_ref.shape[1]
    has_prev = (s > 0).astype(F32)
    has_next = (s < pl.num_programs(1) - 1).astype(F32)
    ext_ref[0:CONV_HALO, :] = prev_ref[0] * has_prev
    ext_ref[CONV_HALO:CONV_HALO + ts, :] = cur_ref[0]
    ext_ref[CONV_HALO + ts:, :] = next_ref[0] * has_next
    off = CONV_HALO - CONV_K // 2
    span = (off + CONV_K - 1) // SUBLANES * SUBLANES
    for c in range(CONV_CH // LANES):
        cs = slice(c * LANES, (c + 1) * LANES)
        for r in range(ts // RC_CONV):
            r0 = r * RC_CONV
            acc = jnp.broadcast_to(cb_ref[:, cs], (RC_CONV, LANES))
            for phase in range(SUBLANES):
                taps = [k for k in range(CONV_K) if (off + k) % SUBLANES == phase]
                shifted = ext_ref[r0 + phase:r0 + phase + RC_CONV + span, cs]
                part = None
                for k in taps:
                    a0 = off + k - phase
                    term = shifted[a0:a0 + RC_CONV, :] * w_ref[k:k + 1, cs]
                    part = term if part is None else part + term
                acc = acc + part
            y_ref[r0:r0 + RC_CONV, cs] = acc
    y = _layer_norm(y_ref[...], g_ref[...], b_ref[...])
    o_ref[0] = (y * jax.nn.sigmoid(y)).astype(BF16)


def _conv_module(u3, conv_w, conv_b, conv_g, conv_bt, layer):
    B, S, C = u3.shape
    ts = TS_CONV
    assert S % ts == 0 and ts % CONV_HALO == 0 and ts % RC_CONV == 0
    halo_per_tile = ts // CONV_HALO
    n_halo = S // CONV_HALO
    vec = pl.BlockSpec((None, 1, C), lambda b, s: (layer, 0, 0))
    return pl.pallas_call(
        _conv_kernel,
        grid=(B, S // ts),
        in_specs=[
            pl.BlockSpec((1, ts, C), lambda b, s: (b, s, 0)),
            pl.BlockSpec((1, CONV_HALO, C), lambda b, s: (b, jnp.maximum(s * halo_per_tile - 1, 0), 0)),
            pl.BlockSpec((1, CONV_HALO, C), lambda b, s: (b, jnp.minimum((s + 1) * halo_per_tile, n_halo - 1), 0)),
            pl.BlockSpec((None, CONV_K, C), lambda b, s: (layer, 0, 0)),
            vec, vec, vec,
        ],
        out_specs=pl.BlockSpec((1, ts, C), lambda b, s: (b, s, 0)),
        out_shape=jax.ShapeDtypeStruct((B, S, C), BF16),
        scratch_shapes=[pltpu.VMEM((ts + 2 * CONV_HALO, C), F32), pltpu.VMEM((ts, C), F32)],
        compiler_params=_params(("parallel", "parallel")),
        name="conv_module",
    )(u3, u3, u3, conv_w, conv_b, conv_g, conv_bt)


def _lane_tree(op, x):
    parts = [x[:, j * LANES:(j + 1) * LANES] for j in range(x.shape[1] // LANES)]
    while len(parts) > 1:
        parts = [op(parts[i], parts[i + 1]) for i in range(0, len(parts) - 1, 2)] + (parts[-1:] if len(parts) % 2 else [])
    return parts[0]


def _exact_zero(x):
    u = pltpu.bitcast(x, jnp.uint32)
    u = lax.shift_right_logical(lax.shift_right_logical(u, jnp.uint32(16)), jnp.uint32(16))
    return pltpu.bitcast(u, F32)


def _attn_kernel(q_ref, k_ref, v_ref, o_ref, sa_ref, sb_ref, ma_ref, mb_ref, p_ref, v1_ref):
    n = pl.program_id(0)
    S = k_ref.shape[1]
    n_groups = S // LANES

    @pl.when(n == 0)
    def _():
        sb_ref[...] = jnp.zeros(sb_ref.shape, F32)
        mb_ref[...] = jnp.zeros(mb_ref.shape, F32)

    @pl.when(jnp.maximum(n - 1, 0) % ((S // TQ) * GQA_GROUP) == 0)
    def _():
        v1_ref[:, 0:HEAD_DIM] = v_ref[0]
        v1_ref[:, HEAD_DIM:] = jnp.ones((S, HEAD_DIM), BF16)

    def step(s_cur, m_cur, s_prev, m_prev):
        q = q_ref[0]
        for c in range(S // TK):
            ks = slice(c * TK, (c + 1) * TK)
            s_cur[:, ks] = lax.dot_general(q, k_ref[0, ks, :], (((1,), (1,)), ((), ())), preferred_element_type=F32)
        for r0 in range(0, TQ, RB_ATTN):
            rs = slice(r0, r0 + RB_ATTN)
            t = s_cur[rs, 0:LANES]
            for j in range(1, n_groups):
                t = jnp.maximum(t, s_cur[rs, j * LANES:(j + 1) * LANES])
            m_cur[rs, :] = jnp.broadcast_to(jnp.max(t, axis=-1, keepdims=True), (RB_ATTN, LANES))
        for r0 in range(0, TQ, RB_ATTN):
            rs = slice(r0, r0 + RB_ATTN)
            m = m_prev[rs, :]
            for j in range(n_groups):
                js = slice(j * LANES, (j + 1) * LANES)
                p_ref[rs, js] = jnp.exp2(s_prev[rs, js] - m).astype(BF16)
        acc = jnp.dot(p_ref[...], v1_ref[...], preferred_element_type=F32)
        o_ref[0] = (acc[:, 0:HEAD_DIM] / acc[:, HEAD_DIM:]).astype(BF16)

    @pl.when(n % 2 == 0)
    def _():
        step(sa_ref, ma_ref, sb_ref, mb_ref)

    @pl.when(n % 2 == 1)
    def _():
        step(sb_ref, mb_ref, sa_ref, ma_ref)


def _attention(qkv3):
    B, S, _ = qkv3.shape
    assert S % TK == 0
    k0 = ATTN_WIDTH // HEAD_DIM
    v0 = (ATTN_WIDTH + KV_WIDTH) // HEAD_DIM
    nq = S // TQ
    n_tiles = B * N_Q_HEADS * nq

    def tile(t):
        return t // (nq * N_Q_HEADS), (t // nq) % N_Q_HEADS, t % nq

    def q_map(n):
        b, h, qi = tile(jnp.minimum(n, n_tiles - 1))
        return b, qi, h

    def k_map(n):
        b, h, _ = tile(jnp.minimum(n, n_tiles - 1))
        return b, 0, k0 + h // GQA_GROUP

    def v_map(n):
        b, h, _ = tile(jnp.maximum(n - 1, 0))
        return b, 0, v0 + h // GQA_GROUP

    def o_map(n):
        b, h, qi = tile(jnp.maximum(n - 1, 0))
        return b, qi, h

    return pl.pallas_call(
        _attn_kernel,
        grid=(n_tiles + 1,),
        in_specs=[
            pl.BlockSpec((1, TQ, HEAD_DIM), q_map),
            pl.BlockSpec((1, S, HEAD_DIM), k_map),
            pl.BlockSpec((1, S, HEAD_DIM), v_map),
        ],
        out_specs=pl.BlockSpec((1, TQ, HEAD_DIM), o_map),
        out_shape=jax.ShapeDtypeStruct((B, S, ATTN_WIDTH), BF16),
        scratch_shapes=[pltpu.VMEM((TQ, S), F32), pltpu.VMEM((TQ, S), F32),
                        pltpu.VMEM((TQ, LANES), F32), pltpu.VMEM((TQ, LANES), F32),
                        pltpu.VMEM((TQ, S), BF16), pltpu.VMEM((S, 2 * HEAD_DIM), BF16)],
        compiler_params=_params(("arbitrary",)),
        name="attention",
    )(qkv3, qkv3, qkv3)


def _out_proj_kernel(attn_ref, conv_ref, w_ref, h_ref, g_ref, b_ref, ho_ref, hbo_ref):
    y = jnp.dot(attn_ref[...], w_ref[0:ATTN_WIDTH, :], preferred_element_type=F32)
    y = y + jnp.dot(conv_ref[...], w_ref[ATTN_WIDTH:, :], preferred_element_type=F32)
    z = _layer_norm(ALPHA * h_ref[...] + y, g_ref[...], b_ref[...])
    ho_ref[...] = z
    hbo_ref[...] = z.astype(BF16)


def _out_proj(attn, conv, w_out_b, h, g, b, layer):
    T, D = h.shape
    row = pl.BlockSpec((TM, D), lambda i: (i, 0))
    half = pl.BlockSpec((TM, ATTN_WIDTH), lambda i: (i, 0))
    vec = pl.BlockSpec((None, 1, D), lambda i: (layer, 0, 0))
    return pl.pallas_call(
        _out_proj_kernel,
        grid=(T // TM,),
        in_specs=[half, half, pl.BlockSpec((None, D, D), lambda i: (layer, 0, 0)), row, vec, vec],
        out_specs=[row, row],
        out_shape=[jax.ShapeDtypeStruct((T, D), F32), jax.ShapeDtypeStruct((T, D), BF16)],
        compiler_params=_params(("parallel",)),
        name="out_proj",
    )(attn, conv, w_out_b, h, g, b)


def _ffn_up_kernel(a_ref, wg_ref, wu_ref, o_ref, wgb_ref, wub_ref):
    i = pl.program_id(1)

    @pl.when(i == 0)
    def _():
        wgb_ref[...] = wg_ref[...].astype(BF16)
        wub_ref[...] = wu_ref[...].astype(BF16)

    a = a_ref[...]
    g = jnp.dot(a, wgb_ref[...], preferred_element_type=F32)
    u = jnp.dot(a, wub_ref[...], preferred_element_type=F32)
    o_ref[...] = (g * jax.nn.sigmoid(g) * u).astype(BF16)


def _ffn_up(hb, w_up, layer):
    T, D = hb.shape
    nf = FF_DIM // TN
    return pl.pallas_call(
        _ffn_up_kernel,
        grid=(nf, T // TM_UP),
        in_specs=[
            pl.BlockSpec((TM_UP, D), lambda f, i: (i, 0)),
            pl.BlockSpec((None, D, TN), lambda f, i: (layer, 0, f)),
            pl.BlockSpec((None, D, TN), lambda f, i: (layer, 0, nf + f)),
        ],
        out_specs=pl.BlockSpec((TM_UP, TN), lambda f, i: (i, f)),
        out_shape=jax.ShapeDtypeStruct((T, FF_DIM), BF16),
        scratch_shapes=[pltpu.VMEM((D, TN), BF16), pltpu.VMEM((D, TN), BF16)],
        compiler_params=_params(("arbitrary", "arbitrary")),
        name="ffn_up",
    )(hb, w_up, w_up)


def _ffn_down_kernel(a_ref, w_ref, h_ref, g_ref, b_ref, ho_ref, hbo_ref, acc_ref):
    k = pl.program_id(1)
    part = jnp.dot(a_ref[...], w_ref[...], preferred_element_type=F32)

    @pl.when(k == 0)
    def _():
        acc_ref[...] = part

    @pl.when(k > 0)
    def _():
        acc_ref[...] += part

    @pl.when(k == pl.num_programs(1) - 1)
    def _():
        z = _layer_norm(ALPHA * h_ref[...] + acc_ref[...], g_ref[...], b_ref[...])
        ho_ref[...] = z
        hbo_ref[...] = z.astype(BF16)


def _ffn_down(act, w_down_b, h, g, b, dense_layer, layer):
    T, D = h.shape
    row = pl.BlockSpec((TM, D), lambda i, k: (i, 0))
    vec = pl.BlockSpec((None, 1, D), lambda i, k: (layer, 0, 0))
    return pl.pallas_call(
        _ffn_down_kernel,
        grid=(T // TM, FF_DIM // TK_DOWN),
        in_specs=[
            pl.BlockSpec((TM, TK_DOWN), lambda i, k: (i, k)),
            pl.BlockSpec((None, TK_DOWN, D), lambda i, k: (dense_layer, k, 0)),
            row, vec, vec,
        ],
        out_specs=[row, row],
        out_shape=[jax.ShapeDtypeStruct((T, D), F32), jax.ShapeDtypeStruct((T, D), BF16)],
        scratch_shapes=[pltpu.VMEM((TM, D), F32)],
        compiler_params=_params(("parallel", "arbitrary")),
        name="ffn_down",
    )(act, w_down_b, h, g, b)


_L_E1, _L_E2, _L_R1, _L_R2, _L_G1, _L_G2 = range(6)


def _router_kernel(h_ref, wr_ref, info_ref, cnt_ref, carry_ref):
    i = pl.program_id(0)

    @pl.when(i == 0)
    def _():
        carry_ref[...] = jnp.zeros_like(carry_ref)

    logits = jnp.dot(h_ref[...], wr_ref[...], preferred_element_type=F32, precision=lax.Precision.HIGHEST)
    lane = lax.broadcasted_iota(jnp.int32, logits.shape, 1).astype(F32)
    neg = jnp.float32(-jnp.inf)
    lg = jnp.where(lane < N_EXPERTS, logits, neg)
    m1 = jnp.max(lg, axis=-1, keepdims=True)
    i1 = jnp.min(jnp.where(lg == m1, lane, float(LANES)), axis=-1, keepdims=True)
    oh1 = lane == i1
    lg2 = jnp.where(oh1, neg, lg)
    m2 = jnp.max(lg2, axis=-1, keepdims=True)
    i2 = jnp.min(jnp.where(lg2 == m2, lane, float(LANES)), axis=-1, keepdims=True)
    oh2 = lane == i2
    e = jnp.exp(m2 - m1)
    g1 = 1.0 / (1.0 + e)
    g2 = e * g1
    cnt = oh1.astype(F32) + oh2.astype(F32)
    tm = cnt.shape[0]
    row = lax.broadcasted_iota(jnp.int32, (tm, tm), 0)
    col = lax.broadcasted_iota(jnp.int32, (tm, tm), 1)
    strict_lower = jnp.where(col < row, 1.0, 0.0).astype(BF16)
    before = jnp.dot(strict_lower, cnt.astype(BF16), preferred_element_type=F32) + carry_ref[...]
    r1 = jnp.sum(jnp.where(oh1, before, 0.0), axis=-1, keepdims=True)
    r2 = jnp.sum(jnp.where(oh2, before, 0.0), axis=-1, keepdims=True)
    carry_ref[...] += jnp.sum(cnt, axis=0, keepdims=True)
    cnt_ref[...] = jnp.broadcast_to(carry_ref[...], cnt_ref.shape)
    info = jnp.zeros_like(logits)
    for idx, val in ((_L_E1, i1), (_L_E2, i2), (_L_R1, r1), (_L_R2, r2), (_L_G1, g1), (_L_G2, g2)):
        info = jnp.where(lane == idx, val, info)
    info_ref[...] = info


def _router(h, w_router_pad):
    T, D = h.shape
    return pl.pallas_call(
        _router_kernel,
        grid=(T // TM,),
        in_specs=[pl.BlockSpec((TM, D), lambda i: (i, 0)), pl.BlockSpec((D, LANES), lambda i: (0, 0))],
        out_specs=[pl.BlockSpec((TM, LANES), lambda i: (i, 0)), pl.BlockSpec((8, LANES), lambda i: (0, 0))],
        out_shape=[jax.ShapeDtypeStruct((T, LANES), F32), jax.ShapeDtypeStruct((8, LANES), F32)],
        scratch_shapes=[pltpu.VMEM((1, LANES), F32)],
        compiler_params=_params(("arbitrary",)),
        name="moe_router",
    )(h, w_router_pad)


def _row_copy(src_hbm, src_row, dst, dst_row, sem):
    return pltpu.make_async_copy(src_hbm.at[pl.ds(src_row, 1), :], dst.at[pl.ds(dst_row, 1), :], sem)


def _dispatch_kernel(nused_ref, tok_ref, h_hbm, o_ref, buf_ref, sem):
    i = pl.program_id(0)
    rows = buf_ref.shape[0]

    @pl.when(i < nused_ref[0])
    def _():
        def issue(r2, c):
            for q in range(2):
                r = 2 * r2 + q
                _row_copy(h_hbm, tok_ref[0, 0, r], buf_ref, r, sem).start(priority=q)
            return c

        lax.fori_loop(0, rows // 2, issue, 0, unroll=4)

        def wait(r, c):
            _row_copy(h_hbm, 0, buf_ref, r, sem).wait()
            return c

        lax.fori_loop(0, rows, wait, 0, unroll=8)
        o_ref[...] = buf_ref[...].astype(BF16)

    @pl.when(i >= nused_ref[0])
    def _():
        o_ref[...] = jnp.zeros_like(o_ref)


def _dispatch(h, buf_tok3, nused):
    T, D = h.shape
    n_blk = buf_tok3.shape[0]
    grid_spec = pltpu.PrefetchScalarGridSpec(
        num_scalar_prefetch=1,
        grid=(n_blk,),
        in_specs=[
            pl.BlockSpec((1, 1, TM_MOE), lambda i, nu: (i, 0, 0), memory_space=pltpu.SMEM),
            pl.BlockSpec(memory_space=pl.ANY),
        ],
        out_specs=pl.BlockSpec((TM_MOE, D), lambda i, nu: (i, 0)),
        scratch_shapes=[pltpu.VMEM((TM_MOE, D), F32), pltpu.SemaphoreType.DMA(())],
    )
    return pl.pallas_call(
        _dispatch_kernel,
        grid_spec=grid_spec,
        out_shape=jax.ShapeDtypeStruct((n_blk * TM_MOE, D), BF16),
        compiler_params=_params(("arbitrary",)),
        name="moe_dispatch",
    )(nused, buf_tok3, h)


def _expert_changed(be_ref, i):
    return jnp.logical_or(i == 0, be_ref[i] != be_ref[jnp.maximum(i - 1, 0)])


def _moe_up_kernel(be_ref, nused_ref, x_ref, wg_ref, wu_ref, o_ref, wgb_ref, wub_ref):
    i = pl.program_id(1)

    @pl.when(_expert_changed(be_ref, i))
    def _():
        wgb_ref[...] = wg_ref[...].astype(BF16)
        wub_ref[...] = wu_ref[...].astype(BF16)

    @pl.when(i < nused_ref[0])
    def _():
        x = x_ref[...]
        g = jnp.dot(x, wgb_ref[...], preferred_element_type=F32)
        u = jnp.dot(x, wub_ref[...], preferred_element_type=F32)
        o_ref[...] = (g * jax.nn.sigmoid(g) * u).astype(BF16)

    @pl.when(i >= nused_ref[0])
    def _():
        o_ref[...] = jnp.zeros_like(o_ref)


def _moe_up(xs, w_up, moe_layer, blk_exp, nused):
    P, D = xs.shape
    nf = FF_DIM // TN
    n_blk = P // TM_MOE
    grid_spec = pltpu.PrefetchScalarGridSpec(
        num_scalar_prefetch=2,
        grid=(nf, n_blk),
        in_specs=[
            pl.BlockSpec((TM_MOE, D), lambda f, i, be, nu: (jnp.minimum(i, jnp.maximum(nu[0] - 1, 0)), 0)),
            pl.BlockSpec((None, None, D, TN), lambda f, i, be, nu: (moe_layer, be[i], 0, f)),
            pl.BlockSpec((None, None, D, TN), lambda f, i, be, nu: (moe_layer, be[i], 0, nf + f)),
        ],
        out_specs=pl.BlockSpec((TM_MOE, TN), lambda f, i, be, nu: (i, f)),
        scratch_shapes=[pltpu.VMEM((D, TN), BF16), pltpu.VMEM((D, TN), BF16)],
    )
    return pl.pallas_call(
        _moe_up_kernel,
        grid_spec=grid_spec,
        out_shape=jax.ShapeDtypeStruct((P, FF_DIM), BF16),
        compiler_params=_params(("arbitrary", "arbitrary")),
        name="moe_up",
    )(blk_exp, nused, xs, w_up, w_up)


def _moe_down_kernel(be_ref, nused_ref, a_ref, w_ref, o_ref, wb_ref):
    i = pl.program_id(1)

    @pl.when(_expert_changed(be_ref, i))
    def _():
        wb_ref[...] = w_ref[...].astype(BF16)

    @pl.when(i < nused_ref[0])
    def _():
        o_ref[...] = jnp.dot(a_ref[...], wb_ref[...], preferred_element_type=F32)

    @pl.when(i >= nused_ref[0])
    def _():
        o_ref[...] = jnp.zeros_like(o_ref)


def _moe_down(act, w_down, moe_layer, blk_exp, nused):
    P, F = act.shape
    n_blk = P // TM_MOE
    grid_spec = pltpu.PrefetchScalarGridSpec(
        num_scalar_prefetch=2,
        grid=(D_MODEL // TN, n_blk),
        in_specs=[
            pl.BlockSpec((TM_MOE, F), lambda j, i, be, nu: (jnp.minimum(i, jnp.maximum(nu[0] - 1, 0)), 0)),
            pl.BlockSpec((None, None, F, TN), lambda j, i, be, nu: (moe_layer, be[i], 0, j)),
        ],
        out_specs=pl.BlockSpec((TM_MOE, TN), lambda j, i, be, nu: (i, j)),
        scratch_shapes=[pltpu.VMEM((F, TN), BF16)],
    )
    return pl.pallas_call(
        _moe_down_kernel,
        grid_spec=grid_spec,
        out_shape=jax.ShapeDtypeStruct((P, D_MODEL), F32),
        compiler_params=_params(("arbitrary", "arbitrary")),
        name="moe_down",
    )(blk_exp, nused, act, w_down)


def _combine_kernel(d1_ref, d2_ref, y_hbm, info_ref, h_ref, g_ref, b_ref, ho_ref, hbo_ref, y1_ref, y2_ref, sem):
    rows = y1_ref.shape[0]

    def issue(r, c):
        _row_copy(y_hbm, d1_ref[0, 0, r], y1_ref, r, sem.at[0]).start(priority=0)
        _row_copy(y_hbm, d2_ref[0, 0, r], y2_ref, r, sem.at[1]).start(priority=1)
        return c

    lax.fori_loop(0, rows, issue, 0, unroll=8)

    def wait(r, c):
        _row_copy(y_hbm, 0, y1_ref, r, sem.at[0]).wait()
        _row_copy(y_hbm, 0, y2_ref, r, sem.at[1]).wait()
        return c

    lax.fori_loop(0, rows, wait, 0, unroll=8)
    info = info_ref[...]
    g1 = info[:, _L_G1:_L_G1 + 1]
    g2 = info[:, _L_G2:_L_G2 + 1]
    y = y1_ref[...] * g1 + y2_ref[...] * g2
    z = _layer_norm(ALPHA * h_ref[...] + y, g_ref[...], b_ref[...])
    ho_ref[...] = z
    hbo_ref[...] = z.astype(BF16)


def _combine(y, dest1, dest2, info, h, g, b, layer):
    T, D = h.shape
    row = pl.BlockSpec((TM, D), lambda i: (i, 0))
    vec = pl.BlockSpec((None, 1, D), lambda i: (layer, 0, 0))
    idx = pl.BlockSpec((1, 1, TM), lambda i: (i, 0, 0), memory_space=pltpu.SMEM)
    return pl.pallas_call(
        _combine_kernel,
        grid=(T // TM,),
        in_specs=[idx, idx, pl.BlockSpec(memory_space=pl.ANY), pl.BlockSpec((TM, LANES), lambda i: (i, 0)), row, vec, vec],
        out_specs=[row, row],
        out_shape=[jax.ShapeDtypeStruct((T, D), F32), jax.ShapeDtypeStruct((T, D), BF16)],
        scratch_shapes=[pltpu.VMEM((TM, D), F32), pltpu.VMEM((TM, D), F32), pltpu.SemaphoreType.DMA((2,))],
        compiler_params=_params(("arbitrary",)),
        name="moe_combine",
    )(dest1.reshape(T // TM, 1, TM), dest2.reshape(T // TM, 1, TM), y, info, h, g, b)


def _moe_ffn(h, w_router, w_up, w_down, moe_layer, ln_g, ln_b, layer):
    T, D = h.shape
    n_blk = (2 * T) // TM_MOE + N_EXPERTS
    w_router_pad = jnp.pad(w_router[moe_layer], ((0, 0), (0, LANES - N_EXPERTS)))
    info, cnt = _router(h, w_router_pad)
    counts = cnt[0, :N_EXPERTS].astype(jnp.int32)
    padded = (counts + TM_MOE - 1) // TM_MOE * TM_MOE
    pad_ends = jnp.cumsum(padded)
    pad_starts = pad_ends - padded
    e1 = info[:, _L_E1].astype(jnp.int32)
    e2 = info[:, _L_E2].astype(jnp.int32)
    dest1 = pad_starts[e1] + info[:, _L_R1].astype(jnp.int32)
    dest2 = pad_starts[e2] + info[:, _L_R2].astype(jnp.int32)
    tok = jnp.arange(T, dtype=jnp.int32)
    buf_tok = jnp.full((n_blk * TM_MOE,), T - 1, jnp.int32).at[jnp.concatenate([dest1, dest2])].set(
        jnp.concatenate([tok, tok]), unique_indices=True, mode="promise_in_bounds")
    nused = (pad_ends[-1] // TM_MOE).astype(jnp.int32).reshape(1)
    blk_start = jnp.minimum(jnp.arange(n_blk, dtype=jnp.int32), jnp.maximum(nused[0] - 1, 0)) * TM_MOE
    blk_exp = jnp.minimum(jnp.searchsorted(pad_ends, blk_start, side="right"), N_EXPERTS - 1).astype(jnp.int32)
    xs = _dispatch(h, buf_tok.reshape(n_blk, 1, TM_MOE), nused)
    act = _moe_up(xs, w_up, moe_layer, blk_exp, nused)
    y = _moe_down(act, w_down, moe_layer, blk_exp, nused)
    return _combine(y, dest1, dest2, info, h, ln_g, ln_b, layer)


def _rope_tables(S):
    t = jnp.arange(S, dtype=jnp.int32)
    inv_freq = ROPE_THETA ** (-jnp.arange(0, ROPE_AXIS_DIM, 2, dtype=F32) / ROPE_AXIS_DIM)
    ang_row = (t // GRID_W).astype(F32)[:, None] * inv_freq[None, :]
    ang_col = (t % GRID_W).astype(F32)[:, None] * inv_freq[None, :]
    cos = jnp.concatenate([jnp.cos(ang_row)] * 2 + [jnp.cos(ang_col)] * 2, axis=-1)
    sin = jnp.concatenate([-jnp.sin(ang_row), jnp.sin(ang_row), -jnp.sin(ang_col), jnp.sin(ang_col)], axis=-1)
    return cos, sin


def kernel(x, ln_in_g, ln_in_b, w_in, q_norm_g, k_norm_g, conv_w, conv_b, conv_ln_g, conv_ln_b, w_out, ln_mix_g, ln_mix_b, ffn_w_up, ffn_w_down, moe_router, moe_w_up, moe_w_down, ln_ffn_g, ln_ffn_b):
    B, S, D = x.shape
    T = B * S
    assert D == D_MODEL and T % TM_UP == 0 and S % TQ == 0
    cos_t, sin_t = _rope_tables(S)
    w_out_b = w_out.astype(BF16)
    ffn_w_down_b = ffn_w_down.astype(BF16)
    q_norm_g, k_norm_g, conv_b, conv_ln_g, conv_ln_b, ln_mix_g, ln_mix_b, ln_ffn_g, ln_ffn_b = (
        p.reshape(DEPTH, 1, -1)
        for p in (q_norm_g, k_norm_g, conv_b, conv_ln_g, conv_ln_b, ln_mix_g, ln_mix_b, ln_ffn_g, ln_ffn_b))
    h, hb = _ln_in(x.reshape(T, D), ln_in_g, ln_in_b)
    for layer in range(DEPTH):
        qkv = _qkv_proj(hb, w_in, layer, cos_t, sin_t, q_norm_g, k_norm_g, S)
        u = _glu_proj(hb, w_in, layer)
        attn = _attention(qkv.reshape(B, S, QKV_COLS)).reshape(T, ATTN_WIDTH)
        conv = _conv_module(u.reshape(B, S, CONV_CH), conv_w, conv_b, conv_ln_g, conv_ln_b, layer).reshape(T, CONV_CH)
        h, hb = _out_proj(attn, conv, w_out_b, h, ln_mix_g, ln_mix_b, layer)
        if layer % 2 == 0:
            act = _ffn_up(hb, ffn_w_up, layer // 2)
            h, hb = _ffn_down(act, ffn_w_down_b, h, ln_ffn_g, ln_ffn_b, layer // 2, layer)
        else:
            h, hb = _moe_ffn(h, moe_router, moe_w_up, moe_w_down, layer // 2, ln_ffn_g, ln_ffn_b, layer)
    return h.reshape(B, S, D)
```

```python
import functools

import jax
import jax.numpy as jnp
from jax import lax
from jax.experimental import pallas as pl
from jax.experimental.pallas import tpu as pltpu

F32 = jnp.float32
BF16 = jnp.bfloat16

D_MODEL = 2048
DEPTH = 4
GRID_W = 64
ATTN_WIDTH = 1024
CONV_CH = 1024
HEAD_DIM = 128
N_Q_HEADS = 8
N_KV_HEADS = 2
GQA_GROUP = 4
KV_WIDTH = 256
QKV_COLS = ATTN_WIDTH + 2 * KV_WIDTH
ROPE_THETA = 10000.0
ROPE_AXIS_DIM = 64
CONV_K = 31
CONV_HALO = 16
FF_DIM = 5632
N_EXPERTS = 8
ALPHA = (2.0 * DEPTH) ** 0.25
LN_EPS = 1e-5
RMS_EPS = 1e-6
LOG2_E = 1.4426950408889634

V7X_VMEM_BYTES = 64 * 1024 * 1024
VMEM_LIMIT = V7X_VMEM_BYTES - 8 * 1024 * 1024
LANES = 128
SUBLANES = 8

TM = 512
TM_PROJ = 1024
TM_UP = 2048
TN = 512
TQ = 1024
TK = 1024
RB_ATTN = 32
TS_CONV = 512
RC_CONV = 128
TM_MOE = 512
TK_DOWN = 1408


def _params(sem, flags=None):
    return pltpu.CompilerParams(dimension_semantics=sem, vmem_limit_bytes=VMEM_LIMIT, flags=flags)


def _layer_norm(z, g, b):
    mu = jnp.mean(z, axis=-1, keepdims=True)
    zc = z - mu
    var = jnp.mean(zc * zc, axis=-1, keepdims=True)
    return zc * lax.rsqrt(var + LN_EPS) * g + b


def _ln_in_kernel(x_ref, g_ref, b_ref, h_ref, hb_ref):
    y = _layer_norm(x_ref[...], g_ref[...], b_ref[...])
    h_ref[...] = y
    hb_ref[...] = y.astype(BF16)


def _ln_in(x2, g, b):
    T, D = x2.shape
    row = pl.BlockSpec((TM, D), lambda i: (i, 0))
    vec = pl.BlockSpec((1, D), lambda i: (0, 0))
    return pl.pallas_call(
        _ln_in_kernel,
        grid=(T // TM,),
        in_specs=[row, vec, vec],
        out_specs=[row, row],
        out_shape=[jax.ShapeDtypeStruct((T, D), F32), jax.ShapeDtypeStruct((T, D), BF16)],
        compiler_params=_params(("parallel",)),
        name="ln_in",
    )(x2, g.reshape(1, D), b.reshape(1, D))


def _pipelined(n, n_tiles, work, finish, buf_a, buf_b, extra=True):
    for parity, (cur, prev) in enumerate(((buf_a, buf_b), (buf_b, buf_a))):
        @pl.when(jnp.logical_and(jnp.logical_and(n % 2 == parity, n < n_tiles), extra))
        def _(cur=cur, prev=prev):
            work(cur)
            finish(prev)

    last = buf_a if (n_tiles - 1) % 2 == 0 else buf_b

    @pl.when(jnp.logical_and(n == n_tiles, extra))
    def _():
        finish(last)


def _qkv_kernel(a_ref, w_ref, cos_ref, sin_ref, gain_ref, flag_ref, o_ref, wb_ref, acc_a, acc_b, *, n_rows, n_steps):
    n = pl.program_id(0)

    @pl.when(n == 0)
    def _():
        acc_b[...] = jnp.zeros(acc_b.shape, F32)

    @pl.when(jnp.logical_and(n % n_rows == 0, n < n_steps))
    def _():
        wb_ref[...] = w_ref[...].astype(BF16)

    lane = lax.broadcasted_iota(jnp.int32, (TM_PROJ, HEAD_DIM), 1)
    first_half = (lane & (ROPE_AXIS_DIM // 2)) == 0

    def work(acc_cur):
        acc_cur[...] = jnp.dot(a_ref[...], wb_ref[...], preferred_element_type=F32)

    def finish(acc_prev):
        cos = cos_ref[...]
        sin = sin_ref[...]
        for c in range(TN // HEAD_DIM):
            cs = slice(c * HEAD_DIM, (c + 1) * HEAD_DIM)
            on = flag_ref[:, cs] > 0.0
            x = acc_prev[:, cs]
            ms = jnp.mean(x * x, axis=-1, keepdims=True)
            xn = x * jnp.where(on, lax.rsqrt(ms + RMS_EPS), 1.0) * gain_ref[:, cs]
            sw = jnp.where(first_half, pltpu.roll(xn, HEAD_DIM - 32, 1), pltpu.roll(xn, 32, 1))
            o_ref[:, cs] = (xn * jnp.where(on, cos, 1.0) + sw * jnp.where(on, sin, 0.0)).astype(BF16)

    _pipelined(n, n_steps, work, finish, acc_a, acc_b)


def _qkv_proj(hb, w_in, layer, cos_t, sin_t, gain3, flag3, S):
    T, D = hb.shape
    assert TN == 2 * KV_WIDTH and ATTN_WIDTH % TN == 0 and S % TM_PROJ == 0
    s_tiles = S // TM_PROJ
    n_rows = T // TM_PROJ
    n_steps = (QKV_COLS // TN) * n_rows

    def cur(n):
        t = jnp.minimum(n, n_steps - 1)
        return t // n_rows, t % n_rows

    def prev(n):
        t = jnp.maximum(n - 1, 0)
        return t // n_rows, t % n_rows

    return pl.pallas_call(
        functools.partial(_qkv_kernel, n_rows=n_rows, n_steps=n_steps),
        grid=(n_steps + 1,),
        in_specs=[
            pl.BlockSpec((TM_PROJ, D), lambda n: (cur(n)[1], 0)),
            pl.BlockSpec((None, D, TN), lambda n: (layer, 0, cur(n)[0])),
            pl.BlockSpec((TM_PROJ, HEAD_DIM), lambda n: (prev(n)[1] % s_tiles, 0)),
            pl.BlockSpec((TM_PROJ, HEAD_DIM), lambda n: (prev(n)[1] % s_tiles, 0)),
            pl.BlockSpec((None, None, 1, TN), lambda n: (layer, prev(n)[0], 0, 0)),
            pl.BlockSpec((None, 1, TN), lambda n: (prev(n)[0], 0, 0)),
        ],
        out_specs=pl.BlockSpec((TM_PROJ, TN), lambda n: (prev(n)[1], prev(n)[0])),
        out_shape=jax.ShapeDtypeStruct((T, QKV_COLS), BF16),
        scratch_shapes=[pltpu.VMEM((D, TN), BF16), pltpu.VMEM((TM_PROJ, TN), F32), pltpu.VMEM((TM_PROJ, TN), F32)],
        compiler_params=_params(("arbitrary",)),
        name="qkv_proj",
    )(hb, w_in, cos_t, sin_t, gain3, flag3)


def _glu_kernel(a_ref, wa_ref, wg_ref, o_ref, wab_ref, wgb_ref):
    i = pl.program_id(1)

    @pl.when(i == 0)
    def _():
        wab_ref[...] = wa_ref[...].astype(BF16)
        wgb_ref[...] = wg_ref[...].astype(BF16)

    a = a_ref[...]
    val = jnp.dot(a, wab_ref[...], preferred_element_type=F32)
    gate = jnp.dot(a, wgb_ref[...], preferred_element_type=F32)
    o_ref[...] = val * jax.nn.sigmoid(gate)


def _glu_proj(hb, w_in, layer):
    T, D = hb.shape
    val0 = QKV_COLS // TN
    gate0 = (QKV_COLS + CONV_CH) // TN
    return pl.pallas_call(
        _glu_kernel,
        grid=(CONV_CH // TN, T // TM_PROJ),
        in_specs=[
            pl.BlockSpec((TM_PROJ, D), lambda j, i: (i, 0)),
            pl.BlockSpec((None, D, TN), lambda j, i: (layer, 0, val0 + j)),
            pl.BlockSpec((None, D, TN), lambda j, i: (layer, 0, gate0 + j)),
        ],
        out_specs=pl.BlockSpec((TM_PROJ, TN), lambda j, i: (i, j)),
        out_shape=jax.ShapeDtypeStruct((T, CONV_CH), F32),
        scratch_shapes=[pltpu.VMEM((D, TN), BF16), pltpu.VMEM((D, TN), BF16)],
        compiler_params=_params(("arbitrary", "arbitrary")),
        name="glu_proj",
    )(hb, w_in, w_in)


def _conv_kernel(cur_ref, prev_ref, next_ref, w_ref, cb_ref, g_ref, b_ref, o_ref, ext_ref, y_ref):
    s = pl.program_id(1)
    ts = cur_ref.shape[1]
    has_prev = (s > 0).astype(F32)
    has_next = (s < pl.num_programs(1) - 1).astype(F32)
    ext_ref[0:CONV_HALO, :] = prev_ref[0] * has_prev
    ext_ref[CONV_HALO:CONV_HALO + ts, :] = cur_ref[0]
    ext_ref[CONV_HALO + ts:, :] = next_ref[0] * has_next
    off = CONV_HALO - CONV_K // 2
    span = (off + CONV_K - 1) // SUBLANES * SUBLANES
    for c in range(CONV_CH // LANES):
        cs = slice(c * LANES, (c + 1) * LANES)
        for r in range(ts // RC_CONV):
            r0 = r * RC_CONV
            acc = jnp.broadcast_to(cb_ref[:, cs], (RC_CONV, LANES))
            for phase in range(SUBLANES):
                taps = [k for k in range(CONV_K) if (off + k) % SUBLANES == phase]
                shifted = ext_ref[r0 + phase:r0 + phase + RC_CONV + span, cs]
                part = None
                for k in taps:
                    a0 = off + k - phase
                    term = shifted[a0:a0 + RC_CONV, :] * w_ref[k:k + 1, cs]
                    part = term if part is None else part + term
                acc = acc + part
            y_ref[r0:r0 + RC_CONV, cs] = acc
    y = _layer_norm(y_ref[...], g_ref[...], b_ref[...])
    o_ref[0] = (y * jax.nn.sigmoid(y)).astype(BF16)


def _conv_module(u3, conv_w, conv_b, conv_g, conv_bt, layer):
    B, S, C = u3.shape
    ts = TS_CONV
    assert S % ts == 0 and ts % CONV_HALO == 0 and ts % RC_CONV == 0
    halo_per_tile = ts // CONV_HALO
    n_halo = S // CONV_HALO
    vec = pl.BlockSpec((None, 1, C), lambda b, s: (layer, 0, 0))
    return pl.pallas_call(
        _conv_kernel,
        grid=(B, S // ts),
        in_specs=[
            pl.BlockSpec((1, ts, C), lambda b, s: (b, s, 0)),
            pl.BlockSpec((1, CONV_HALO, C), lambda b, s: (b, jnp.maximum(s * halo_per_tile - 1, 0), 0)),
            pl.BlockSpec((1, CONV_HALO, C), lambda b, s: (b, jnp.minimum((s + 1) * halo_per_tile, n_halo - 1), 0)),
            pl.BlockSpec((None, CONV_K, C), lambda b, s: (layer, 0, 0)),
            vec, vec, vec,
        ],
        out_specs=pl.BlockSpec((1, ts, C), lambda b, s: (b, s, 0)),
        out_shape=jax.ShapeDtypeStruct((B, S, C), BF16),
        scratch_shapes=[pltpu.VMEM((ts + 2 * CONV_HALO, C), F32), pltpu.VMEM((ts, C), F32)],
        compiler_params=_params(("parallel", "parallel")),
        name="conv_module",
    )(u3, u3, u3, conv_w, conv_b, conv_g, conv_bt)


def _lane_tree(op, x):
    parts = [x[:, j * LANES:(j + 1) * LANES] for j in range(x.shape[1] // LANES)]
    while len(parts) > 1:
        parts = [op(parts[i], parts[i + 1]) for i in range(0, len(parts) - 1, 2)] + (parts[-1:] if len(parts) % 2 else [])
    return parts[0]


def _exact_zero(x):
    u = pltpu.bitcast(x, jnp.uint32)
    u = lax.shift_right_logical(lax.shift_right_logical(u, jnp.uint32(16)), jnp.uint32(16))
    return pltpu.bitcast(u, F32)


def _attn_kernel(q_ref, k_ref, v_ref, o_ref, sa_ref, sb_ref, ma_ref, mb_ref, p_ref, v1_ref):
    n = pl.program_id(0)
    S = k_ref.shape[1]
    n_groups = S // LANES

    @pl.when(n == 0)
    def _():
        sb_ref[...] = jnp.zeros(sb_ref.shape, F32)
        mb_ref[...] = jnp.zeros(mb_ref.shape, F32)

    @pl.when(jnp.maximum(n - 1, 0) % ((S // TQ) * GQA_GROUP) == 0)
    def _():
        v1_ref[:, 0:HEAD_DIM] = v_ref[0]
        v1_ref[:, HEAD_DIM:] = jnp.ones((S, HEAD_DIM), BF16)

    def step(s_cur, m_cur, s_prev, m_prev):
        q = q_ref[0]
        for c in range(S // TK):
            ks = slice(c * TK, (c + 1) * TK)
            s_cur[:, ks] = lax.dot_general(q, k_ref[0, ks, :], (((1,), (1,)), ((), ())), preferred_element_type=F32)
        for r0 in range(0, TQ, RB_ATTN):
            rs = slice(r0, r0 + RB_ATTN)
            t = s_cur[rs, 0:LANES]
            for j in range(1, n_groups):
                t = jnp.maximum(t, s_cur[rs, j * LANES:(j + 1) * LANES])
            m_cur[rs, :] = jnp.broadcast_to(jnp.max(t, axis=-1, keepdims=True), (RB_ATTN, LANES))
        for r0 in range(0, TQ, RB_ATTN):
            rs = slice(r0, r0 + RB_ATTN)
            m = m_prev[rs, :]
            for j in range(n_groups):
                js = slice(j * LANES, (j + 1) * LANES)
                p_ref[rs, js] = jnp.exp2(s_prev[rs, js] - m).astype(BF16)
        acc = jnp.dot(p_ref[...], v1_ref[...], preferred_element_type=F32)
        o_ref[0] = (acc[:, 0:HEAD_DIM] / acc[:, HEAD_DIM:]).astype(BF16)

    @pl.when(n % 2 == 0)
    def _():
        step(sa_ref, ma_ref, sb_ref, mb_ref)

    @pl.when(n % 2 == 1)
    def _():
        step(sb_ref, mb_ref, sa_ref, ma_ref)


def _attention(qkv3):
    B, S, _ = qkv3.shape
    assert S % TK == 0
    k0 = ATTN_WIDTH // HEAD_DIM
    v0 = (ATTN_WIDTH + KV_WIDTH) // HEAD_DIM
    nq = S // TQ
    n_tiles = B * N_Q_HEADS * nq

    def tile(t):
        return t // (nq * N_Q_HEADS), (t // nq) % N_Q_HEADS, t % nq

    def q_map(n):
        b, h, qi = tile(jnp.minimum(n, n_tiles - 1))
        return b, qi, h

    def k_map(n):
        b, h, _ = tile(jnp.minimum(n, n_tiles - 1))
        return b, 0, k0 + h // GQA_GROUP

    def v_map(n):
        b, h, _ = tile(jnp.maximum(n - 1, 0))
        return b, 0, v0 + h // GQA_GROUP

    def o_map(n):
        b, h, qi = tile(jnp.maximum(n - 1, 0))
        return b, qi, h

    return pl.pallas_call(
        _attn_kernel,
        grid=(n_tiles + 1,),
        in_specs=[
            pl.BlockSpec((1, TQ, HEAD_DIM), q_map),
            pl.BlockSpec((1, S, HEAD_DIM), k_map),
            pl.BlockSpec((1, S, HEAD_DIM), v_map),
        ],
        out_specs=pl.BlockSpec((1, TQ, HEAD_DIM), o_map),
        out_shape=jax.ShapeDtypeStruct((B, S, ATTN_WIDTH), BF16),
        scratch_shapes=[pltpu.VMEM((TQ, S), F32), pltpu.VMEM((TQ, S), F32),
                        pltpu.VMEM((TQ, LANES), F32), pltpu.VMEM((TQ, LANES), F32),
                        pltpu.VMEM((TQ, S), BF16), pltpu.VMEM((S, 2 * HEAD_DIM), BF16)],
        compiler_params=_params(("arbitrary",)),
        name="attention",
    )(qkv3, qkv3, qkv3)


def _out_proj_kernel(attn_ref, conv_ref, w_ref, h_ref, g_ref, b_ref, ho_ref, hbo_ref):
    y = jnp.dot(attn_ref[...], w_ref[0:ATTN_WIDTH, :], preferred_element_type=F32)
    y = y + jnp.dot(conv_ref[...], w_ref[ATTN_WIDTH:, :], preferred_element_type=F32)
    z = _layer_norm(ALPHA * h_ref[...] + y, g_ref[...], b_ref[...])
    ho_ref[...] = z
    hbo_ref[...] = z.astype(BF16)


def _out_proj(attn, conv, w_out_b, h, g, b, layer):
    T, D = h.shape
    row = pl.BlockSpec((TM, D), lambda i: (i, 0))
    half = pl.BlockSpec((TM, ATTN_WIDTH), lambda i: (i, 0))
    vec = pl.BlockSpec((None, 1, D), lambda i: (layer, 0, 0))
    return pl.pallas_call(
        _out_proj_kernel,
        grid=(T // TM,),
        in_specs=[half, half, pl.BlockSpec((None, D, D), lambda i: (layer, 0, 0)), row, vec, vec],
        out_specs=[row, row],
        out_shape=[jax.ShapeDtypeStruct((T, D), F32), jax.ShapeDtypeStruct((T, D), BF16)],
        compiler_params=_params(("parallel",)),
        name="out_proj",
    )(attn, conv, w_out_b, h, g, b)


def _ffn_up_kernel(a_ref, wg_ref, wu_ref, o_ref, wgb_ref, wub_ref):
    i = pl.program_id(1)

    @pl.when(i == 0)
    def _():
        wgb_ref[...] = wg_ref[...].astype(BF16)
        wub_ref[...] = wu_ref[...].astype(BF16)

    a = a_ref[...]
    g = jnp.dot(a, wgb_ref[...], preferred_element_type=F32)
    u = jnp.dot(a, wub_ref[...], preferred_element_type=F32)
    o_ref[...] = (g * jax.nn.sigmoid(g) * u).astype(BF16)


def _ffn_up(hb, w_up, layer):
    T, D = hb.shape
    nf = FF_DIM // TN
    return pl.pallas_call(
        _ffn_up_kernel,
        grid=(nf, T // TM_UP),
        in_specs=[
            pl.BlockSpec((TM_UP, D), lambda f, i: (i, 0)),
            pl.BlockSpec((None, D, TN), lambda f, i: (layer, 0, f)),
            pl.BlockSpec((None, D, TN), lambda f, i: (layer, 0, nf + f)),
        ],
        out_specs=pl.BlockSpec((TM_UP, TN), lambda f, i: (i, f)),
        out_shape=jax.ShapeDtypeStruct((T, FF_DIM), BF16),
        scratch_shapes=[pltpu.VMEM((D, TN), BF16), pltpu.VMEM((D, TN), BF16)],
        compiler_params=_params(("arbitrary", "arbitrary")),
        name="ffn_up",
    )(hb, w_up, w_up)


def _ffn_down_kernel(a_ref, w_ref, h_ref, g_ref, b_ref, ho_ref, hbo_ref, acc_ref):
    k = pl.program_id(1)
    part = jnp.dot(a_ref[...], w_ref[...], preferred_element_type=F32)

    @pl.when(k == 0)
    def _():
        acc_ref[...] = part

    @pl.when(k > 0)
    def _():
        acc_ref[...] += part

    @pl.when(k == pl.num_programs(1) - 1)
    def _():
        z = _layer_norm(ALPHA * h_ref[...] + acc_ref[...], g_ref[...], b_ref[...])
        ho_ref[...] = z
        hbo_ref[...] = z.astype(BF16)


def _ffn_down(act, w_down_b, h, g, b, dense_layer, layer):
    T, D = h.shape
    row = pl.BlockSpec((TM, D), lambda i, k: (i, 0))
    vec = pl.BlockSpec((None, 1, D), lambda i, k: (layer, 0, 0))
    return pl.pallas_call(
        _ffn_down_kernel,
        grid=(T // TM, FF_DIM // TK_DOWN),
        in_specs=[
            pl.BlockSpec((TM, TK_DOWN), lambda i, k: (i, k)),
            pl.BlockSpec((None, TK_DOWN, D), lambda i, k: (dense_layer, k, 0)),
            row, vec, vec,
        ],
        out_specs=[row, row],
        out_shape=[jax.ShapeDtypeStruct((T, D), F32), jax.ShapeDtypeStruct((T, D), BF16)],
        scratch_shapes=[pltpu.VMEM((TM, D), F32)],
        compiler_params=_params(("parallel", "arbitrary")),
        name="ffn_down",
    )(act, w_down_b, h, g, b)


_L_E1, _L_E2, _L_R1, _L_R2, _L_G1, _L_G2 = range(6)


def _router_kernel(h_ref, wr_ref, info_ref, cnt_ref, carry_ref):
    i = pl.program_id(0)

    @pl.when(i == 0)
    def _():
        carry_ref[...] = jnp.zeros_like(carry_ref)

    logits = jnp.dot(h_ref[...], wr_ref[...], preferred_element_type=F32, precision=lax.Precision.HIGHEST)
    lane = lax.broadcasted_iota(jnp.int32, logits.shape, 1).astype(F32)
    neg = jnp.float32(-jnp.inf)
    lg = jnp.where(lane < N_EXPERTS, logits, neg)
    m1 = jnp.max(lg, axis=-1, keepdims=True)
    i1 = jnp.min(jnp.where(lg == m1, lane, float(LANES)), axis=-1, keepdims=True)
    oh1 = lane == i1
    lg2 = jnp.where(oh1, neg, lg)
    m2 = jnp.max(lg2, axis=-1, keepdims=True)
    i2 = jnp.min(jnp.where(lg2 == m2, lane, float(LANES)), axis=-1, keepdims=True)
    oh2 = lane == i2
    e = jnp.exp(m2 - m1)
    g1 = 1.0 / (1.0 + e)
    g2 = e * g1
    cnt = oh1.astype(F32) + oh2.astype(F32)
    tm = cnt.shape[0]
    row = lax.broadcasted_iota(jnp.int32, (tm, tm), 0)
    col = lax.broadcasted_iota(jnp.int32, (tm, tm), 1)
    strict_lower = jnp.where(col < row, 1.0, 0.0).astype(BF16)
    before = jnp.dot(strict_lower, cnt.astype(BF16), preferred_element_type=F32) + carry_ref[...]
    r1 = jnp.sum(jnp.where(oh1, before, 0.0), axis=-1, keepdims=True)
    r2 = jnp.sum(jnp.where(oh2, before, 0.0), axis=-1, keepdims=True)
    carry_ref[...] += jnp.sum(cnt, axis=0, keepdims=True)
    cnt_ref[...] = jnp.broadcast_to(carry_ref[...], cnt_ref.shape)
    info = jnp.zeros_like(logits)
    for idx, val in ((_L_E1, i1), (_L_E2, i2), (_L_R1, r1), (_L_R2, r2), (_L_G1, g1), (_L_G2, g2)):
        info = jnp.where(lane == idx, val, info)
    info_ref[...] = info


def _router(h, w_router_pad):
    T, D = h.shape
    return pl.pallas_call(
        _router_kernel,
        grid=(T // TM,),
        in_specs=[pl.BlockSpec((TM, D), lambda i: (i, 0)), pl.BlockSpec((D, LANES), lambda i: (0, 0))],
        out_specs=[pl.BlockSpec((TM, LANES), lambda i: (i, 0)), pl.BlockSpec((8, LANES), lambda i: (0, 0))],
        out_shape=[jax.ShapeDtypeStruct((T, LANES), F32), jax.ShapeDtypeStruct((8, LANES), F32)],
        scratch_shapes=[pltpu.VMEM((1, LANES), F32)],
        compiler_params=_params(("arbitrary",)),
        name="moe_router",
    )(h, w_router_pad)


def _row_copy(src_hbm, src_row, dst, dst_row, sem):
    return pltpu.make_async_copy(src_hbm.at[pl.ds(src_row, 1), :], dst.at[pl.ds(dst_row, 1), :], sem)


def _dispatch_kernel(nused_ref, tok_ref, h_hbm, o_ref, buf_ref, sem):
    i = pl.program_id(0)
    rows = buf_ref.shape[0]

    @pl.when(i < nused_ref[0])
    def _():
        def issue(r2, c):
            for q in range(2):
                r = 2 * r2 + q
                _row_copy(h_hbm, tok_ref[0, 0, r], buf_ref, r, sem).start(priority=q)
            return c

        lax.fori_loop(0, rows // 2, issue, 0, unroll=4)

        def wait(r, c):
            _row_copy(h_hbm, 0, buf_ref, r, sem).wait()
            return c

        lax.fori_loop(0, rows, wait, 0, unroll=8)
        o_ref[...] = buf_ref[...].astype(BF16)

    @pl.when(i >= nused_ref[0])
    def _():
        o_ref[...] = jnp.zeros_like(o_ref)


def _dispatch(h, buf_tok3, nused):
    T, D = h.shape
    n_blk = buf_tok3.shape[0]
    grid_spec = pltpu.PrefetchScalarGridSpec(
        num_scalar_prefetch=1,
        grid=(n_blk,),
        in_specs=[
            pl.BlockSpec((1, 1, TM_MOE), lambda i, nu: (i, 0, 0), memory_space=pltpu.SMEM),
            pl.BlockSpec(memory_space=pl.ANY),
        ],
        out_specs=pl.BlockSpec((TM_MOE, D), lambda i, nu: (i, 0)),
        scratch_shapes=[pltpu.VMEM((TM_MOE, D), F32), pltpu.SemaphoreType.DMA(())],
    )
    return pl.pallas_call(
        _dispatch_kernel,
        grid_spec=grid_spec,
        out_shape=jax.ShapeDtypeStruct((n_blk * TM_MOE, D), BF16),
        compiler_params=_params(("arbitrary",)),
        name="moe_dispatch",
    )(nused, buf_tok3, h)


def _expert_changed(be_ref, i):
    return jnp.logical_or(i == 0, be_ref[i] != be_ref[jnp.maximum(i - 1, 0)])


def _moe_up_kernel(be_ref, nused_ref, x_ref, wg_ref, wu_ref, o_ref, wgb_ref, wub_ref):
    i = pl.program_id(1)

    @pl.when(_expert_changed(be_ref, i))
    def _():
        wgb_ref[...] = wg_ref[...].astype(BF16)
        wub_ref[...] = wu_ref[...].astype(BF16)

    @pl.when(i < nused_ref[0])
    def _():
        x = x_ref[...]
        g = jnp.dot(x, wgb_ref[...], preferred_element_type=F32)
        u = jnp.dot(x, wub_ref[...], preferred_element_type=F32)
        o_ref[...] = (g * jax.nn.sigmoid(g) * u).astype(BF16)

    @pl.when(i >= nused_ref[0])
    def _():
        o_ref[...] = jnp.zeros_like(o_ref)


def _moe_up(xs, w_up, moe_layer, blk_exp, nused):
    P, D = xs.shape
    nf = FF_DIM // TN
    n_blk = P // TM_MOE
    grid_spec = pltpu.PrefetchScalarGridSpec(
        num_scalar_prefetch=2,
        grid=(nf, n_blk),
        in_specs=[
            pl.BlockSpec((TM_MOE, D), lambda f, i, be, nu: (jnp.minimum(i, jnp.maximum(nu[0] - 1, 0)), 0)),
            pl.BlockSpec((None, None, D, TN), lambda f, i, be, nu: (moe_layer, be[i], 0, f)),
            pl.BlockSpec((None, None, D, TN), lambda f, i, be, nu: (moe_layer, be[i], 0, nf + f)),
        ],
        out_specs=pl.BlockSpec((TM_MOE, TN), lambda f, i, be, nu: (i, f)),
        scratch_shapes=[pltpu.VMEM((D, TN), BF16), pltpu.VMEM((D, TN), BF16)],
    )
    return pl.pallas_call(
        _moe_up_kernel,
        grid_spec=grid_spec,
        out_shape=jax.ShapeDtypeStruct((P, FF_DIM), BF16),
        compiler_params=_params(("arbitrary", "arbitrary")),
        name="moe_up",
    )(blk_exp, nused, xs, w_up, w_up)


def _moe_down_kernel(be_ref, nused_ref, a_ref, w_ref, o_ref, wb_ref):
    i = pl.program_id(1)

    @pl.when(_expert_changed(be_ref, i))
    def _():
        wb_ref[...] = w_ref[...].astype(BF16)

    @pl.when(i < nused_ref[0])
    def _():
        o_ref[...] = jnp.dot(a_ref[...], wb_ref[...], preferred_element_type=F32)

    @pl.when(i >= nused_ref[0])
    def _():
        o_ref[...] = jnp.zeros_like(o_ref)


def _moe_down(act, w_down, moe_layer, blk_exp, nused):
    P, F = act.shape
    n_blk = P // TM_MOE
    grid_spec = pltpu.PrefetchScalarGridSpec(
        num_scalar_prefetch=2,
        grid=(D_MODEL // TN, n_blk),
        in_specs=[
            pl.BlockSpec((TM_MOE, F), lambda j, i, be, nu: (jnp.minimum(i, jnp.maximum(nu[0] - 1, 0)), 0)),
            pl.BlockSpec((None, None, F, TN), lambda j, i, be, nu: (moe_layer, be[i], 0, j)),
        ],
        out_specs=pl.BlockSpec((TM_MOE, TN), lambda j, i, be, nu: (i, j)),
        scratch_shapes=[pltpu.VMEM((F, TN), BF16)],
    )
    return pl.pallas_call(
        _moe_down_kernel,
        grid_spec=grid_spec,
        out_shape=jax.ShapeDtypeStruct((P, D_MODEL), F32),
        compiler_params=_params(("arbitrary", "arbitrary")),
        name="moe_down",
    )(blk_exp, nused, act, w_down)


def _combine_kernel(d1_ref, d2_ref, y_hbm, info_ref, h_ref, g_ref, b_ref, ho_ref, hbo_ref, y1_ref, y2_ref, sem):
    rows = y1_ref.shape[0]

    def issue(r, c):
        _row_copy(y_hbm, d1_ref[0, 0, r], y1_ref, r, sem.at[0]).start(priority=0)
        _row_copy(y_hbm, d2_ref[0, 0, r], y2_ref, r, sem.at[1]).start(priority=1)
        return c

    lax.fori_loop(0, rows, issue, 0, unroll=8)

    def wait(r, c):
        _row_copy(y_hbm, 0, y1_ref, r, sem.at[0]).wait()
        _row_copy(y_hbm, 0, y2_ref, r, sem.at[1]).wait()
        return c

    lax.fori_loop(0, rows, wait, 0, unroll=8)
    info = info_ref[...]
    g1 = info[:, _L_G1:_L_G1 + 1]
    g2 = info[:, _L_G2:_L_G2 + 1]
    y = y1_ref[...] * g1 + y2_ref[...] * g2
    z = _layer_norm(ALPHA * h_ref[...] + y, g_ref[...], b_ref[...])
    ho_ref[...] = z
    hbo_ref[...] = z.astype(BF16)


def _combine(y, dest1, dest2, info, h, g, b, layer):
    T, D = h.shape
    row = pl.BlockSpec((TM, D), lambda i: (i, 0))
    vec = pl.BlockSpec((None, 1, D), lambda i: (layer, 0, 0))
    idx = pl.BlockSpec((1, 1, TM), lambda i: (i, 0, 0), memory_space=pltpu.SMEM)
    return pl.pallas_call(
        _combine_kernel,
        grid=(T // TM,),
        in_specs=[idx, idx, pl.BlockSpec(memory_space=pl.ANY), pl.BlockSpec((TM, LANES), lambda i: (i, 0)), row, vec, vec],
        out_specs=[row, row],
        out_shape=[jax.ShapeDtypeStruct((T, D), F32), jax.ShapeDtypeStruct((T, D), BF16)],
        scratch_shapes=[pltpu.VMEM((TM, D), F32), pltpu.VMEM((TM, D), F32), pltpu.SemaphoreType.DMA((2,))],
        compiler_params=_params(("arbitrary",)),
        name="moe_combine",
    )(dest1.reshape(T // TM, 1, TM), dest2.reshape(T // TM, 1, TM), y, info, h, g, b)


def _moe_ffn(h, w_router, w_up, w_down, moe_layer, ln_g, ln_b, layer):
    T, D = h.shape
    n_blk = (2 * T) // TM_MOE + N_EXPERTS
    w_router_pad = jnp.pad(w_router[moe_layer], ((0, 0), (0, LANES - N_EXPERTS)))
    info, cnt = _router(h, w_router_pad)
    counts = cnt[0, :N_EXPERTS].astype(jnp.int32)
    padded = (counts + TM_MOE - 1) // TM_MOE * TM_MOE
    pad_ends = jnp.cumsum(padded)
    pad_starts = pad_ends - padded
    e1 = info[:, _L_E1].astype(jnp.int32)
    e2 = info[:, _L_E2].astype(jnp.int32)
    dest1 = pad_starts[e1] + info[:, _L_R1].astype(jnp.int32)
    dest2 = pad_starts[e2] + info[:, _L_R2].astype(jnp.int32)
    tok = jnp.arange(T, dtype=jnp.int32)
    buf_tok = jnp.full((n_blk * TM_MOE,), T - 1, jnp.int32).at[jnp.concatenate([dest1, dest2])].set(
        jnp.concatenate([tok, tok]), unique_indices=True, mode="promise_in_bounds")
    nused = (pad_ends[-1] // TM_MOE).astype(jnp.int32).reshape(1)
    blk_start = jnp.minimum(jnp.arange(n_blk, dtype=jnp.int32), jnp.maximum(nused[0] - 1, 0)) * TM_MOE
    blk_exp = jnp.minimum(jnp.searchsorted(pad_ends, blk_start, side="right"), N_EXPERTS - 1).astype(jnp.int32)
    xs = _dispatch(h, buf_tok.reshape(n_blk, 1, TM_MOE), nused)
    act = _moe_up(xs, w_up, moe_layer, blk_exp, nused)
    y = _moe_down(act, w_down, moe_layer, blk_exp, nused)
    return _combine(y, dest1, dest2, info, h, ln_g, ln_b, layer)


def _rope_tables(S):
    t = jnp.arange(S, dtype=jnp.int32)
    inv_freq = ROPE_THETA ** (-jnp.arange(0, ROPE_AXIS_DIM, 2, dtype=F32) / ROPE_AXIS_DIM)
    ang_row = (t // GRID_W).astype(F32)[:, None] * inv_freq[None, :]
    ang_col = (t % GRID_W).astype(F32)[:, None] * inv_freq[None, :]
    cos = jnp.concatenate([jnp.cos(ang_row)] * 2 + [jnp.cos(ang_col)] * 2, axis=-1)
    sin = jnp.concatenate([-jnp.sin(ang_row), jnp.sin(ang_row), -jnp.sin(ang_col), jnp.sin(ang_col)], axis=-1)
    return cos, sin


def kernel(x, ln_in_g, ln_in_b, w_in, q_norm_g, k_norm_g, conv_w, conv_b, conv_ln_g, conv_ln_b, w_out, ln_mix_g, ln_mix_b, ffn_w_up, ffn_w_down, moe_router, moe_w_up, moe_w_down, ln_ffn_g, ln_ffn_b):
    B, S, D = x.shape
    T = B * S
    assert D == D_MODEL and T % TM_UP == 0 and S % TQ == 0
    cos_t, sin_t = _rope_tables(S)
    w_out_b = w_out.astype(BF16)
    ffn_w_down_b = ffn_w_down.astype(BF16)
    conv_b, conv_ln_g, conv_ln_b, ln_mix_g, ln_mix_b, ln_ffn_g, ln_ffn_b = (
        p.reshape(DEPTH, 1, -1)
        for p in (conv_b, conv_ln_g, conv_ln_b, ln_mix_g, ln_mix_b, ln_ffn_g, ln_ffn_b))
    heads_per_tile = TN // HEAD_DIM
    q_gain = jnp.tile(q_norm_g * (HEAD_DIM ** -0.5 * LOG2_E), (1, heads_per_tile))
    kv_gain = jnp.concatenate([jnp.tile(k_norm_g, (1, N_KV_HEADS)), jnp.ones((DEPTH, KV_WIDTH), F32)], axis=1)
    qkv_gain = jnp.stack([q_gain] * (ATTN_WIDTH // TN) + [kv_gain], axis=1).reshape(DEPTH, -1, 1, TN)
    kv_flag = jnp.concatenate([jnp.ones((KV_WIDTH,), F32), jnp.zeros((KV_WIDTH,), F32)])
    qkv_flag = jnp.stack([jnp.ones((TN,), F32)] * (ATTN_WIDTH // TN) + [kv_flag]).reshape(-1, 1, TN)
    h, hb = _ln_in(x.reshape(T, D), ln_in_g, ln_in_b)
    for layer in range(DEPTH):
        qkv = _qkv_proj(hb, w_in, layer, cos_t, sin_t, qkv_gain, qkv_flag, S)
        u = _glu_proj(hb, w_in, layer)
        attn = _attention(qkv.reshape(B, S, QKV_COLS)).reshape(T, ATTN_WIDTH)
        conv = _conv_module(u.reshape(B, S, CONV_CH), conv_w, conv_b, conv_ln_g, conv_ln_b, layer).reshape(T, CONV_CH)
        h, hb = _out_proj(attn, conv, w_out_b, h, ln_mix_g, ln_mix_b, layer)
        if layer % 2 == 0:
            act = _ffn_up(hb, ffn_w_up, layer // 2)
            h, hb = _ffn_down(act, ffn_w_down_b, h, ln_ffn_g, ln_ffn_b, layer // 2, layer)
        else:
            h, hb = _moe_ffn(h, moe_router, moe_w_up, moe_w_down, layer // 2, ln_ffn_g, ln_ffn_b, layer)
    return h.reshape(B, S, D)
```

```python
import functools

import jax
import jax.numpy as jnp
from jax import lax
from jax.experimental import pallas as pl
from jax.experimental.pallas import tpu as pltpu

F32 = jnp.float32
BF16 = jnp.bfloat16

D_MODEL = 2048
DEPTH = 4
GRID_W = 64
ATTN_WIDTH = 1024
CONV_CH = 1024
HEAD_DIM = 128
N_Q_HEADS = 8
N_KV_HEADS = 2
GQA_GROUP = 4
KV_WIDTH = 256
QKV_COLS = ATTN_WIDTH + 2 * KV_WIDTH
ROPE_THETA = 10000.0
ROPE_AXIS_DIM = 64
CONV_K = 31
CONV_HALO = 16
FF_DIM = 5632
N_EXPERTS = 8
ALPHA = (2.0 * DEPTH) ** 0.25
LN_EPS = 1e-5
RMS_EPS = 1e-6
LOG2_E = 1.4426950408889634

V7X_VMEM_BYTES = 64 * 1024 * 1024
VMEM_LIMIT = V7X_VMEM_BYTES - 8 * 1024 * 1024
LANES = 128
SUBLANES = 8

TM = 512
TM_PROJ = 1024
TM_UP = 1024
TN = 512
TQ = 1024
TK = 1024
RB_ATTN = 32
TS_CONV = 512
RC_CONV = 128
TM_MOE = 512
TK_DOWN = 1408


def _params(sem, flags=None):
    return pltpu.CompilerParams(dimension_semantics=sem, vmem_limit_bytes=VMEM_LIMIT, flags=flags)


def _layer_norm(z, g, b):
    mu = jnp.mean(z, axis=-1, keepdims=True)
    zc = z - mu
    var = jnp.mean(zc * zc, axis=-1, keepdims=True)
    return zc * lax.rsqrt(var + LN_EPS) * g + b


def _ln_in_kernel(x_ref, g_ref, b_ref, h_ref, hb_ref):
    y = _layer_norm(x_ref[...], g_ref[...], b_ref[...])
    h_ref[...] = y
    hb_ref[...] = y.astype(BF16)


def _ln_in(x2, g, b):
    T, D = x2.shape
    row = pl.BlockSpec((TM, D), lambda i: (i, 0))
    vec = pl.BlockSpec((1, D), lambda i: (0, 0))
    return pl.pallas_call(
        _ln_in_kernel,
        grid=(T // TM,),
        in_specs=[row, vec, vec],
        out_specs=[row, row],
        out_shape=[jax.ShapeDtypeStruct((T, D), F32), jax.ShapeDtypeStruct((T, D), BF16)],
        compiler_params=_params(("parallel",)),
        name="ln_in",
    )(x2, g.reshape(1, D), b.reshape(1, D))


def _pipelined(n, n_tiles, work, finish, buf_a, buf_b, extra=True):
    for parity, (cur, prev) in enumerate(((buf_a, buf_b), (buf_b, buf_a))):
        @pl.when(jnp.logical_and(jnp.logical_and(n % 2 == parity, n < n_tiles), extra))
        def _(cur=cur, prev=prev):
            work(cur)
            finish(prev)

    last = buf_a if (n_tiles - 1) % 2 == 0 else buf_b

    @pl.when(jnp.logical_and(n == n_tiles, extra))
    def _():
        finish(last)


def _qkv_kernel(a_ref, w_ref, cos_ref, sin_ref, gain_ref, flag_ref, o_ref, wb_ref, acc_a, acc_b, *, n_rows, n_steps):
    n = pl.program_id(0)

    @pl.when(n == 0)
    def _():
        acc_b[...] = jnp.zeros(acc_b.shape, F32)

    @pl.when(jnp.logical_and(n % n_rows == 0, n < n_steps))
    def _():
        wb_ref[...] = w_ref[...].astype(BF16)

    lane = lax.broadcasted_iota(jnp.int32, (TM_PROJ, HEAD_DIM), 1)
    first_half = (lane & (ROPE_AXIS_DIM // 2)) == 0

    def work(acc_cur):
        acc_cur[...] = jnp.dot(a_ref[...], wb_ref[...], preferred_element_type=F32)

    def finish(acc_prev):
        cos = cos_ref[...]
        sin = sin_ref[...]
        for c in range(TN // HEAD_DIM):
            cs = slice(c * HEAD_DIM, (c + 1) * HEAD_DIM)
            on = flag_ref[:, cs] > 0.0
            x = acc_prev[:, cs]
            ms = jnp.mean(x * x, axis=-1, keepdims=True)
            xn = x * jnp.where(on, lax.rsqrt(ms + RMS_EPS), 1.0) * gain_ref[:, cs]
            sw = jnp.where(first_half, pltpu.roll(xn, HEAD_DIM - 32, 1), pltpu.roll(xn, 32, 1))
            o_ref[:, cs] = (xn * jnp.where(on, cos, 1.0) + sw * jnp.where(on, sin, 0.0)).astype(BF16)

    _pipelined(n, n_steps, work, finish, acc_a, acc_b)


def _qkv_proj(hb, w_in, layer, cos_t, sin_t, gain3, flag3, S):
    T, D = hb.shape
    assert TN == 2 * KV_WIDTH and ATTN_WIDTH % TN == 0 and S % TM_PROJ == 0
    s_tiles = S // TM_PROJ
    n_rows = T // TM_PROJ
    n_steps = (QKV_COLS // TN) * n_rows

    def cur(n):
        t = jnp.minimum(n, n_steps - 1)
        return t // n_rows, t % n_rows

    def prev(n):
        t = jnp.maximum(n - 1, 0)
        return t // n_rows, t % n_rows

    return pl.pallas_call(
        functools.partial(_qkv_kernel, n_rows=n_rows, n_steps=n_steps),
        grid=(n_steps + 1,),
        in_specs=[
            pl.BlockSpec((TM_PROJ, D), lambda n: (cur(n)[1], 0)),
            pl.BlockSpec((None, D, TN), lambda n: (layer, 0, cur(n)[0])),
            pl.BlockSpec((TM_PROJ, HEAD_DIM), lambda n: (prev(n)[1] % s_tiles, 0)),
            pl.BlockSpec((TM_PROJ, HEAD_DIM), lambda n: (prev(n)[1] % s_tiles, 0)),
            pl.BlockSpec((None, None, 1, TN), lambda n: (layer, prev(n)[0], 0, 0)),
            pl.BlockSpec((None, 1, TN), lambda n: (prev(n)[0], 0, 0)),
        ],
        out_specs=pl.BlockSpec((TM_PROJ, TN), lambda n: (prev(n)[1], prev(n)[0])),
        out_shape=jax.ShapeDtypeStruct((T, QKV_COLS), BF16),
        scratch_shapes=[pltpu.VMEM((D, TN), BF16), pltpu.VMEM((TM_PROJ, TN), F32), pltpu.VMEM((TM_PROJ, TN), F32)],
        compiler_params=_params(("arbitrary",)),
        name="qkv_proj",
    )(hb, w_in, cos_t, sin_t, gain3, flag3)


def _glu_kernel(a_ref, wa_ref, wg_ref, o_ref, wab_ref, wgb_ref):
    i = pl.program_id(1)

    @pl.when(i == 0)
    def _():
        wab_ref[...] = wa_ref[...].astype(BF16)
        wgb_ref[...] = wg_ref[...].astype(BF16)

    a = a_ref[...]
    val = jnp.dot(a, wab_ref[...], preferred_element_type=F32)
    gate = jnp.dot(a, wgb_ref[...], preferred_element_type=F32)
    o_ref[...] = val * jax.nn.sigmoid(gate)


def _glu_proj(hb, w_in, layer):
    T, D = hb.shape
    val0 = QKV_COLS // TN
    gate0 = (QKV_COLS + CONV_CH) // TN
    return pl.pallas_call(
        _glu_kernel,
        grid=(CONV_CH // TN, T // TM_PROJ),
        in_specs=[
            pl.BlockSpec((TM_PROJ, D), lambda j, i: (i, 0)),
            pl.BlockSpec((None, D, TN), lambda j, i: (layer, 0, val0 + j)),
            pl.BlockSpec((None, D, TN), lambda j, i: (layer, 0, gate0 + j)),
        ],
        out_specs=pl.BlockSpec((TM_PROJ, TN), lambda j, i: (i, j)),
        out_shape=jax.ShapeDtypeStruct((T, CONV_CH), F32),
        scratch_shapes=[pltpu.VMEM((D, TN), BF16), pltpu.VMEM((D, TN), BF16)],
        compiler_params=_params(("arbitrary", "arbitrary")),
        name="glu_proj",
    )(hb, w_in, w_in)


def _conv_kernel(cur_ref, prev_ref, next_ref, w_ref, cb_ref, g_ref, b_ref, o_ref, ext_ref, y_ref):
    s = pl.program_id(1)
    ts = cur_ref.shape[1]
    has_prev = (s > 0).astype(F32)
    has_next = (s < pl.num_programs(1) - 1).astype(F32)
    ext_ref[0:CONV_HALO, :] = prev_ref[0] * has_prev
    ext_ref[CONV_HALO:CONV_HALO + ts, :] = cur_ref[0]
    ext_ref[CONV_HALO + ts:, :] = next_ref[0] * has_next
    off = CONV_HALO - CONV_K // 2
    span = (off + CONV_K - 1) // SUBLANES * SUBLANES
    for c in range(CONV_CH // LANES):
        cs = slice(c * LANES, (c + 1) * LANES)
        for r in range(ts // RC_CONV):
            r0 = r * RC_CONV
            acc = jnp.broadcast_to(cb_ref[:, cs], (RC_CONV, LANES))
            for phase in range(SUBLANES):
                taps = [k for k in range(CONV_K) if (off + k) % SUBLANES == phase]
                shifted = ext_ref[r0 + phase:r0 + phase + RC_CONV + span, cs]
                part = None
                for k in taps:
                    a0 = off + k - phase
                    term = shifted[a0:a0 + RC_CONV, :] * w_ref[k:k + 1, cs]
                    part = term if part is None else part + term
                acc = acc + part
            y_ref[r0:r0 + RC_CONV, cs] = acc
    y = _layer_norm(y_ref[...], g_ref[...], b_ref[...])
    o_ref[0] = (y * jax.nn.sigmoid(y)).astype(BF16)


def _conv_module(u3, conv_w, conv_b, conv_g, conv_bt, layer):
    B, S, C = u3.shape
    ts = TS_CONV
    assert S % ts == 0 and ts % CONV_HALO == 0 and ts % RC_CONV == 0
    halo_per_tile = ts // CONV_HALO
    n_halo = S // CONV_HALO
    vec = pl.BlockSpec((None, 1, C), lambda b, s: (layer, 0, 0))
    return pl.pallas_call(
        _conv_kernel,
        grid=(B, S // ts),
        in_specs=[
            pl.BlockSpec((1, ts, C), lambda b, s: (b, s, 0)),
            pl.BlockSpec((1, CONV_HALO, C), lambda b, s: (b, jnp.maximum(s * halo_per_tile - 1, 0), 0)),
            pl.BlockSpec((1, CONV_HALO, C), lambda b, s: (b, jnp.minimum((s + 1) * halo_per_tile, n_halo - 1), 0)),
            pl.BlockSpec((None, CONV_K, C), lambda b, s: (layer, 0, 0)),
            vec, vec, vec,
        ],
        out_specs=pl.BlockSpec((1, ts, C), lambda b, s: (b, s, 0)),
        out_shape=jax.ShapeDtypeStruct((B, S, C), BF16),
        scratch_shapes=[pltpu.VMEM((ts + 2 * CONV_HALO, C), F32), pltpu.VMEM((ts, C), F32)],
        compiler_params=_params(("parallel", "parallel")),
        name="conv_module",
    )(u3, u3, u3, conv_w, conv_b, conv_g, conv_bt)


def _lane_tree(op, x):
    parts = [x[:, j * LANES:(j + 1) * LANES] for j in range(x.shape[1] // LANES)]
    while len(parts) > 1:
        parts = [op(parts[i], parts[i + 1]) for i in range(0, len(parts) - 1, 2)] + (parts[-1:] if len(parts) % 2 else [])
    return parts[0]


def _exact_zero(x):
    u = pltpu.bitcast(x, jnp.uint32)
    u = lax.shift_right_logical(lax.shift_right_logical(u, jnp.uint32(16)), jnp.uint32(16))
    return pltpu.bitcast(u, F32)


def _attn_kernel(q_ref, k_ref, v_ref, o_ref, sa_ref, sb_ref, ma_ref, mb_ref, p_ref, v1_ref):
    n = pl.program_id(0)
    S = k_ref.shape[1]
    n_groups = S // LANES

    @pl.when(n == 0)
    def _():
        sb_ref[...] = jnp.zeros(sb_ref.shape, F32)
        mb_ref[...] = jnp.zeros(mb_ref.shape, F32)

    @pl.when(jnp.maximum(n - 1, 0) % ((S // TQ) * GQA_GROUP) == 0)
    def _():
        v1_ref[:, 0:HEAD_DIM] = v_ref[0]
        v1_ref[:, HEAD_DIM:] = jnp.ones((S, HEAD_DIM), BF16)

    def step(s_cur, m_cur, s_prev, m_prev):
        q = q_ref[0]
        for c in range(S // TK):
            ks = slice(c * TK, (c + 1) * TK)
            s_cur[:, ks] = lax.dot_general(q, k_ref[0, ks, :], (((1,), (1,)), ((), ())), preferred_element_type=F32)
        for r0 in range(0, TQ, RB_ATTN):
            rs = slice(r0, r0 + RB_ATTN)
            t = s_cur[rs, 0:LANES]
            for j in range(1, n_groups):
                t = jnp.maximum(t, s_cur[rs, j * LANES:(j + 1) * LANES])
            m_cur[rs, :] = jnp.broadcast_to(jnp.max(t, axis=-1, keepdims=True), (RB_ATTN, LANES))
        for r0 in range(0, TQ, RB_ATTN):
            rs = slice(r0, r0 + RB_ATTN)
            m = m_prev[rs, :]
            for j in range(n_groups):
                js = slice(j * LANES, (j + 1) * LANES)
                p_ref[rs, js] = jnp.exp2(s_prev[rs, js] - m).astype(BF16)
        acc = jnp.dot(p_ref[...], v1_ref[...], preferred_element_type=F32)
        o_ref[0] = (acc[:, 0:HEAD_DIM] / acc[:, HEAD_DIM:]).astype(BF16)

    @pl.when(n % 2 == 0)
    def _():
        step(sa_ref, ma_ref, sb_ref, mb_ref)

    @pl.when(n % 2 == 1)
    def _():
        step(sb_ref, mb_ref, sa_ref, ma_ref)


def _attention(qkv3):
    B, S, _ = qkv3.shape
    assert S % TK == 0
    k0 = ATTN_WIDTH // HEAD_DIM
    v0 = (ATTN_WIDTH + KV_WIDTH) // HEAD_DIM
    nq = S // TQ
    n_tiles = B * N_Q_HEADS * nq

    def tile(t):
        return t // (nq * N_Q_HEADS), (t // nq) % N_Q_HEADS, t % nq

    def q_map(n):
        b, h, qi = tile(jnp.minimum(n, n_tiles - 1))
        return b, qi, h

    def k_map(n):
        b, h, _ = tile(jnp.minimum(n, n_tiles - 1))
        return b, 0, k0 + h // GQA_GROUP

    def v_map(n):
        b, h, _ = tile(jnp.maximum(n - 1, 0))
        return b, 0, v0 + h // GQA_GROUP

    def o_map(n):
        b, h, qi = tile(jnp.maximum(n - 1, 0))
        return b, qi, h

    return pl.pallas_call(
        _attn_kernel,
        grid=(n_tiles + 1,),
        in_specs=[
            pl.BlockSpec((1, TQ, HEAD_DIM), q_map),
            pl.BlockSpec((1, S, HEAD_DIM), k_map),
            pl.BlockSpec((1, S, HEAD_DIM), v_map),
        ],
        out_specs=pl.BlockSpec((1, TQ, HEAD_DIM), o_map),
        out_shape=jax.ShapeDtypeStruct((B, S, ATTN_WIDTH), BF16),
        scratch_shapes=[pltpu.VMEM((TQ, S), F32), pltpu.VMEM((TQ, S), F32),
                        pltpu.VMEM((TQ, LANES), F32), pltpu.VMEM((TQ, LANES), F32),
                        pltpu.VMEM((TQ, S), BF16), pltpu.VMEM((S, 2 * HEAD_DIM), BF16)],
        compiler_params=_params(("arbitrary",)),
        name="attention",
    )(qkv3, qkv3, qkv3)


def _out_proj_kernel(attn_ref, conv_ref, w_ref, h_ref, g_ref, b_ref, ho_ref, hbo_ref):
    y = jnp.dot(attn_ref[...], w_ref[0:ATTN_WIDTH, :], preferred_element_type=F32)
    y = y + jnp.dot(conv_ref[...], w_ref[ATTN_WIDTH:, :], preferred_element_type=F32)
    z = _layer_norm(ALPHA * h_ref[...] + y, g_ref[...], b_ref[...])
    ho_ref[...] = z
    hbo_ref[...] = z.astype(BF16)


def _out_proj(attn, conv, w_out_b, h, g, b, layer):
    T, D = h.shape
    row = pl.BlockSpec((TM, D), lambda i: (i, 0))
    half = pl.BlockSpec((TM, ATTN_WIDTH), lambda i: (i, 0))
    vec = pl.BlockSpec((None, 1, D), lambda i: (layer, 0, 0))
    return pl.pallas_call(
        _out_proj_kernel,
        grid=(T // TM,),
        in_specs=[half, half, pl.BlockSpec((None, D, D), lambda i: (layer, 0, 0)), row, vec, vec],
        out_specs=[row, row],
        out_shape=[jax.ShapeDtypeStruct((T, D), F32), jax.ShapeDtypeStruct((T, D), BF16)],
        compiler_params=_params(("parallel",)),
        name="out_proj",
    )(attn, conv, w_out_b, h, g, b)


def _ffn_up_kernel(a_ref, wg_ref, wu_ref, o_ref, wgb_ref, wub_ref):
    i = pl.program_id(1)

    @pl.when(i == 0)
    def _():
        wgb_ref[...] = wg_ref[...].astype(BF16)
        wub_ref[...] = wu_ref[...].astype(BF16)

    a = a_ref[...]
    g = jnp.dot(a, wgb_ref[...], preferred_element_type=F32)
    u = jnp.dot(a, wub_ref[...], preferred_element_type=F32)
    o_ref[...] = (g * jax.nn.sigmoid(g) * u).astype(BF16)


def _ffn_up(hb, w_up, layer):
    T, D = hb.shape
    nf = FF_DIM // TN
    return pl.pallas_call(
        _ffn_up_kernel,
        grid=(nf, T // TM_UP),
        in_specs=[
            pl.BlockSpec((TM_UP, D), lambda f, i: (i, 0)),
            pl.BlockSpec((None, D, TN), lambda f, i: (layer, 0, f)),
            pl.BlockSpec((None, D, TN), lambda f, i: (layer, 0, nf + f)),
        ],
        out_specs=pl.BlockSpec((TM_UP, TN), lambda f, i: (i, f)),
        out_shape=jax.ShapeDtypeStruct((T, FF_DIM), BF16),
        scratch_shapes=[pltpu.VMEM((D, TN), BF16), pltpu.VMEM((D, TN), BF16)],
        compiler_params=_params(("arbitrary", "arbitrary")),
        name="ffn_up",
    )(hb, w_up, w_up)


def _ffn_down_kernel(a_ref, w_ref, h_ref, g_ref, b_ref, ho_ref, hbo_ref, acc_ref):
    k = pl.program_id(1)
    part = jnp.dot(a_ref[...], w_ref[...], preferred_element_type=F32)

    @pl.when(k == 0)
    def _():
        acc_ref[...] = part

    @pl.when(k > 0)
    def _():
        acc_ref[...] += part

    @pl.when(k == pl.num_programs(1) - 1)
    def _():
        z = _layer_norm(ALPHA * h_ref[...] + acc_ref[...], g_ref[...], b_ref[...])
        ho_ref[...] = z
        hbo_ref[...] = z.astype(BF16)


def _ffn_down(act, w_down_b, h, g, b, dense_layer, layer):
    T, D = h.shape
    row = pl.BlockSpec((TM, D), lambda i, k: (i, 0))
    vec = pl.BlockSpec((None, 1, D), lambda i, k: (layer, 0, 0))
    return pl.pallas_call(
        _ffn_down_kernel,
        grid=(T // TM, FF_DIM // TK_DOWN),
        in_specs=[
            pl.BlockSpec((TM, TK_DOWN), lambda i, k: (i, k)),
            pl.BlockSpec((None, TK_DOWN, D), lambda i, k: (dense_layer, k, 0)),
            row, vec, vec,
        ],
        out_specs=[row, row],
        out_shape=[jax.ShapeDtypeStruct((T, D), F32), jax.ShapeDtypeStruct((T, D), BF16)],
        scratch_shapes=[pltpu.VMEM((TM, D), F32)],
        compiler_params=_params(("parallel", "arbitrary")),
        name="ffn_down",
    )(act, w_down_b, h, g, b)


_L_E1, _L_E2, _L_R1, _L_R2, _L_G1, _L_G2 = range(6)


def _split_bf16(x):
    hi = x.astype(BF16)
    return hi, (x - hi.astype(F32)).astype(BF16)


def _router_kernel(h_ref, w1_ref, w2_ref, info_ref, cnt_ref, carry_ref):
    i = pl.program_id(0)

    @pl.when(i == 0)
    def _():
        carry_ref[...] = jnp.zeros_like(carry_ref)

    hi, lo = _split_bf16(h_ref[...])
    d1 = jnp.dot(hi, w1_ref[...], preferred_element_type=F32)
    d2 = jnp.dot(lo, w2_ref[...], preferred_element_type=F32)
    logits = d1 + pltpu.roll(d1, LANES - N_EXPERTS, 1) + d2
    lane = lax.broadcasted_iota(jnp.int32, logits.shape, 1).astype(F32)
    neg = jnp.float32(-jnp.inf)
    lg = jnp.where(lane < N_EXPERTS, logits, neg)
    m1 = jnp.max(lg, axis=-1, keepdims=True)
    i1 = jnp.min(jnp.where(lg == m1, lane, float(LANES)), axis=-1, keepdims=True)
    oh1 = lane == i1
    lg2 = jnp.where(oh1, neg, lg)
    m2 = jnp.max(lg2, axis=-1, keepdims=True)
    i2 = jnp.min(jnp.where(lg2 == m2, lane, float(LANES)), axis=-1, keepdims=True)
    oh2 = lane == i2
    e = jnp.exp(m2 - m1)
    g1 = 1.0 / (1.0 + e)
    g2 = e * g1
    cnt = oh1.astype(F32) + oh2.astype(F32)
    tm = cnt.shape[0]
    row = lax.broadcasted_iota(jnp.int32, (tm, tm), 0)
    col = lax.broadcasted_iota(jnp.int32, (tm, tm), 1)
    strict_lower = jnp.where(col < row, 1.0, 0.0).astype(BF16)
    before = jnp.dot(strict_lower, cnt.astype(BF16), preferred_element_type=F32) + carry_ref[...]
    r1 = jnp.sum(jnp.where(oh1, before, 0.0), axis=-1, keepdims=True)
    r2 = jnp.sum(jnp.where(oh2, before, 0.0), axis=-1, keepdims=True)
    carry_ref[...] += jnp.sum(cnt, axis=0, keepdims=True)
    cnt_ref[...] = jnp.broadcast_to(carry_ref[...], cnt_ref.shape)
    info = jnp.zeros_like(logits)
    for idx, val in ((_L_E1, i1), (_L_E2, i2), (_L_R1, r1), (_L_R2, r2), (_L_G1, g1), (_L_G2, g2)):
        info = jnp.where(lane == idx, val, info)
    info_ref[...] = info


def _router(h, w_router):
    T, D = h.shape
    whi, wlo = _split_bf16(w_router)
    w1 = jnp.pad(jnp.concatenate([whi, wlo], axis=1), ((0, 0), (0, LANES - 2 * N_EXPERTS)))
    w2 = jnp.pad(whi, ((0, 0), (0, LANES - N_EXPERTS)))
    wspec = pl.BlockSpec((D, LANES), lambda i: (0, 0))
    return pl.pallas_call(
        _router_kernel,
        grid=(T // TM,),
        in_specs=[pl.BlockSpec((TM, D), lambda i: (i, 0)), wspec, wspec],
        out_specs=[pl.BlockSpec((TM, LANES), lambda i: (i, 0)), pl.BlockSpec((8, LANES), lambda i: (0, 0))],
        out_shape=[jax.ShapeDtypeStruct((T, LANES), F32), jax.ShapeDtypeStruct((8, LANES), F32)],
        scratch_shapes=[pltpu.VMEM((1, LANES), F32)],
        compiler_params=_params(("arbitrary",)),
        name="moe_router",
    )(h, w1, w2)


def _row_copy(src_hbm, src_row, dst, dst_row, sem):
    return pltpu.make_async_copy(src_hbm.at[pl.ds(src_row, 1), :], dst.at[pl.ds(dst_row, 1), :], sem)


def _dispatch_kernel(nused_ref, tok_ref, h_hbm, o_ref, buf_ref, sem):
    i = pl.program_id(0)
    rows = buf_ref.shape[0]

    @pl.when(i < nused_ref[0])
    def _():
        def issue(r2, c):
            for q in range(2):
                r = 2 * r2 + q
                _row_copy(h_hbm, tok_ref[0, 0, r], buf_ref, r, sem).start(priority=q)
            return c

        lax.fori_loop(0, rows // 2, issue, 0, unroll=4)

        def wait(r, c):
            _row_copy(h_hbm, 0, buf_ref, r, sem).wait()
            return c

        lax.fori_loop(0, rows, wait, 0, unroll=8)
        o_ref[...] = buf_ref[...].astype(BF16)

    @pl.when(i >= nused_ref[0])
    def _():
        o_ref[...] = jnp.zeros_like(o_ref)


def _dispatch(h, buf_tok3, nused):
    T, D = h.shape
    n_blk = buf_tok3.shape[0]
    grid_spec = pltpu.PrefetchScalarGridSpec(
        num_scalar_prefetch=1,
        grid=(n_blk,),
        in_specs=[
            pl.BlockSpec((1, 1, TM_MOE), lambda i, nu: (i, 0, 0), memory_space=pltpu.SMEM),
            pl.BlockSpec(memory_space=pl.ANY),
        ],
        out_specs=pl.BlockSpec((TM_MOE, D), lambda i, nu: (i, 0)),
        scratch_shapes=[pltpu.VMEM((TM_MOE, D), F32), pltpu.SemaphoreType.DMA(())],
    )
    return pl.pallas_call(
        _dispatch_kernel,
        grid_spec=grid_spec,
        out_shape=jax.ShapeDtypeStruct((n_blk * TM_MOE, D), BF16),
        compiler_params=_params(("arbitrary",)),
        name="moe_dispatch",
    )(nused, buf_tok3, h)


def _expert_changed(be_ref, i):
    return jnp.logical_or(i == 0, be_ref[i] != be_ref[jnp.maximum(i - 1, 0)])


def _moe_up_kernel(be_ref, nused_ref, x_ref, wg_ref, wu_ref, o_ref, wgb_ref, wub_ref):
    i = pl.program_id(1)

    @pl.when(_expert_changed(be_ref, i))
    def _():
        wgb_ref[...] = wg_ref[...].astype(BF16)
        wub_ref[...] = wu_ref[...].astype(BF16)

    @pl.when(i < nused_ref[0])
    def _():
        x = x_ref[...]
        g = jnp.dot(x, wgb_ref[...], preferred_element_type=F32)
        u = jnp.dot(x, wub_ref[...], preferred_element_type=F32)
        o_ref[...] = (g * jax.nn.sigmoid(g) * u).astype(BF16)

    @pl.when(i >= nused_ref[0])
    def _():
        o_ref[...] = jnp.zeros_like(o_ref)


def _moe_up(xs, w_up, moe_layer, blk_exp, nused):
    P, D = xs.shape
    nf = FF_DIM // TN
    n_blk = P // TM_MOE
    grid_spec = pltpu.PrefetchScalarGridSpec(
        num_scalar_prefetch=2,
        grid=(nf, n_blk),
        in_specs=[
            pl.BlockSpec((TM_MOE, D), lambda f, i, be, nu: (jnp.minimum(i, jnp.maximum(nu[0] - 1, 0)), 0)),
            pl.BlockSpec((None, None, D, TN), lambda f, i, be, nu: (moe_layer, be[i], 0, f)),
            pl.BlockSpec((None, None, D, TN), lambda f, i, be, nu: (moe_layer, be[i], 0, nf + f)),
        ],
        out_specs=pl.BlockSpec((TM_MOE, TN), lambda f, i, be, nu: (i, f)),
        scratch_shapes=[pltpu.VMEM((D, TN), BF16), pltpu.VMEM((D, TN), BF16)],
    )
    return pl.pallas_call(
        _moe_up_kernel,
        grid_spec=grid_spec,
        out_shape=jax.ShapeDtypeStruct((P, FF_DIM), BF16),
        compiler_params=_params(("arbitrary", "arbitrary")),
        name="moe_up",
    )(blk_exp, nused, xs, w_up, w_up)


def _moe_down_kernel(be_ref, nused_ref, a_ref, w_ref, o_ref, wb_ref):
    i = pl.program_id(1)

    @pl.when(_expert_changed(be_ref, i))
    def _():
        wb_ref[...] = w_ref[...].astype(BF16)

    @pl.when(i < nused_ref[0])
    def _():
        o_ref[...] = jnp.dot(a_ref[...], wb_ref[...], preferred_element_type=F32)

    @pl.when(i >= nused_ref[0])
    def _():
        o_ref[...] = jnp.zeros_like(o_ref)


def _moe_down(act, w_down, moe_layer, blk_exp, nused):
    P, F = act.shape
    n_blk = P // TM_MOE
    grid_spec = pltpu.PrefetchScalarGridSpec(
        num_scalar_prefetch=2,
        grid=(D_MODEL // TN, n_blk),
        in_specs=[
            pl.BlockSpec((TM_MOE, F), lambda j, i, be, nu: (jnp.minimum(i, jnp.maximum(nu[0] - 1, 0)), 0)),
            pl.BlockSpec((None, None, F, TN), lambda j, i, be, nu: (moe_layer, be[i], 0, j)),
        ],
        out_specs=pl.BlockSpec((TM_MOE, TN), lambda j, i, be, nu: (i, j)),
        scratch_shapes=[pltpu.VMEM((F, TN), BF16)],
    )
    return pl.pallas_call(
        _moe_down_kernel,
        grid_spec=grid_spec,
        out_shape=jax.ShapeDtypeStruct((P, D_MODEL), F32),
        compiler_params=_params(("arbitrary", "arbitrary")),
        name="moe_down",
    )(blk_exp, nused, act, w_down)


def _combine_kernel(d1_ref, d2_ref, y_hbm, info_ref, h_ref, g_ref, b_ref, ho_ref, hbo_ref, y1_ref, y2_ref, sem):
    rows = y1_ref.shape[0]

    def issue(r, c):
        _row_copy(y_hbm, d1_ref[0, 0, r], y1_ref, r, sem.at[0]).start(priority=0)
        _row_copy(y_hbm, d2_ref[0, 0, r], y2_ref, r, sem.at[1]).start(priority=1)
        return c

    lax.fori_loop(0, rows, issue, 0, unroll=8)

    def wait(r, c):
        _row_copy(y_hbm, 0, y1_ref, r, sem.at[0]).wait()
        _row_copy(y_hbm, 0, y2_ref, r, sem.at[1]).wait()
        return c

    lax.fori_loop(0, rows, wait, 0, unroll=8)
    info = info_ref[...]
    g1 = info[:, _L_G1:_L_G1 + 1]
    g2 = info[:, _L_G2:_L_G2 + 1]
    y = y1_ref[...] * g1 + y2_ref[...] * g2
    z = _layer_norm(ALPHA * h_ref[...] + y, g_ref[...], b_ref[...])
    ho_ref[...] = z
    hbo_ref[...] = z.astype(BF16)


def _combine(y, dest1, dest2, info, h, g, b, layer):
    T, D = h.shape
    row = pl.BlockSpec((TM, D), lambda i: (i, 0))
    vec = pl.BlockSpec((None, 1, D), lambda i: (layer, 0, 0))
    idx = pl.BlockSpec((1, 1, TM), lambda i: (i, 0, 0), memory_space=pltpu.SMEM)
    return pl.pallas_call(
        _combine_kernel,
        grid=(T // TM,),
        in_specs=[idx, idx, pl.BlockSpec(memory_space=pl.ANY), pl.BlockSpec((TM, LANES), lambda i: (i, 0)), row, vec, vec],
        out_specs=[row, row],
        out_shape=[jax.ShapeDtypeStruct((T, D), F32), jax.ShapeDtypeStruct((T, D), BF16)],
        scratch_shapes=[pltpu.VMEM((TM, D), F32), pltpu.VMEM((TM, D), F32), pltpu.SemaphoreType.DMA((2,))],
        compiler_params=_params(("arbitrary",)),
        name="moe_combine",
    )(dest1.reshape(T // TM, 1, TM), dest2.reshape(T // TM, 1, TM), y, info, h, g, b)


def _moe_ffn(h, w_router, w_up, w_down, moe_layer, ln_g, ln_b, layer):
    T, D = h.shape
    n_blk = (2 * T) // TM_MOE + N_EXPERTS
    info, cnt = _router(h, w_router[moe_layer])
    counts = cnt[0, :N_EXPERTS].astype(jnp.int32)
    padded = (counts + TM_MOE - 1) // TM_MOE * TM_MOE
    pad_ends = jnp.cumsum(padded)
    pad_starts = pad_ends - padded
    e1 = info[:, _L_E1].astype(jnp.int32)
    e2 = info[:, _L_E2].astype(jnp.int32)
    dest1 = pad_starts[e1] + info[:, _L_R1].astype(jnp.int32)
    dest2 = pad_starts[e2] + info[:, _L_R2].astype(jnp.int32)
    tok = jnp.arange(T, dtype=jnp.int32)
    buf_tok = jnp.full((n_blk * TM_MOE,), T - 1, jnp.int32).at[jnp.concatenate([dest1, dest2])].set(
        jnp.concatenate([tok, tok]), unique_indices=True, mode="promise_in_bounds")
    nused = (pad_ends[-1] // TM_MOE).astype(jnp.int32).reshape(1)
    blk_start = jnp.minimum(jnp.arange(n_blk, dtype=jnp.int32), jnp.maximum(nused[0] - 1, 0)) * TM_MOE
    blk_exp = jnp.minimum(jnp.sum(blk_start[:, None] >= pad_ends[None, :], axis=1), N_EXPERTS - 1).astype(jnp.int32)
    xs = _dispatch(h, buf_tok.reshape(n_blk, 1, TM_MOE), nused)
    act = _moe_up(xs, w_up, moe_layer, blk_exp, nused)
    y = _moe_down(act, w_down, moe_layer, blk_exp, nused)
    return _combine(y, dest1, dest2, info, h, ln_g, ln_b, layer)


def _rope_tables(S):
    t = jnp.arange(S, dtype=jnp.int32)
    inv_freq = ROPE_THETA ** (-jnp.arange(0, ROPE_AXIS_DIM, 2, dtype=F32) / ROPE_AXIS_DIM)
    ang_row = (t // GRID_W).astype(F32)[:, None] * inv_freq[None, :]
    ang_col = (t % GRID_W).astype(F32)[:, None] * inv_freq[None, :]
    cos = jnp.concatenate([jnp.cos(ang_row)] * 2 + [jnp.cos(ang_col)] * 2, axis=-1)
    sin = jnp.concatenate([-jnp.sin(ang_row), jnp.sin(ang_row), -jnp.sin(ang_col), jnp.sin(ang_col)], axis=-1)
    return cos, sin


def kernel(x, ln_in_g, ln_in_b, w_in, q_norm_g, k_norm_g, conv_w, conv_b, conv_ln_g, conv_ln_b, w_out, ln_mix_g, ln_mix_b, ffn_w_up, ffn_w_down, moe_router, moe_w_up, moe_w_down, ln_ffn_g, ln_ffn_b):
    B, S, D = x.shape
    T = B * S
    assert D == D_MODEL and T % TM_UP == 0 and S % TQ == 0
    cos_t, sin_t = _rope_tables(S)
    w_out_b = w_out.astype(BF16)
    ffn_w_down_b = ffn_w_down.astype(BF16)
    conv_b, conv_ln_g, conv_ln_b, ln_mix_g, ln_mix_b, ln_ffn_g, ln_ffn_b = (
        p.reshape(DEPTH, 1, -1)
        for p in (conv_b, conv_ln_g, conv_ln_b, ln_mix_g, ln_mix_b, ln_ffn_g, ln_ffn_b))
    heads_per_tile = TN // HEAD_DIM
    q_gain = jnp.tile(q_norm_g * (HEAD_DIM ** -0.5 * LOG2_E), (1, heads_per_tile))
    kv_gain = jnp.concatenate([jnp.tile(k_norm_g, (1, N_KV_HEADS)), jnp.ones((DEPTH, KV_WIDTH), F32)], axis=1)
    qkv_gain = jnp.stack([q_gain] * (ATTN_WIDTH // TN) + [kv_gain], axis=1).reshape(DEPTH, -1, 1, TN)
    kv_flag = jnp.concatenate([jnp.ones((KV_WIDTH,), F32), jnp.zeros((KV_WIDTH,), F32)])
    qkv_flag = jnp.stack([jnp.ones((TN,), F32)] * (ATTN_WIDTH // TN) + [kv_flag]).reshape(-1, 1, TN)
    h, hb = _ln_in(x.reshape(T, D), ln_in_g, ln_in_b)
    for layer in range(DEPTH):
        qkv = _qkv_proj(hb, w_in, layer, cos_t, sin_t, qkv_gain, qkv_flag, S)
        u = _glu_proj(hb, w_in, layer)
        attn = _attention(qkv.reshape(B, S, QKV_COLS)).reshape(T, ATTN_WIDTH)
        conv = _conv_module(u.reshape(B, S, CONV_CH), conv_w, conv_b, conv_ln_g, conv_ln_b, layer).reshape(T, CONV_CH)
        h, hb = _out_proj(attn, conv, w_out_b, h, ln_mix_g, ln_mix_b, layer)
        if layer % 2 == 0:
            act = _ffn_up(hb, ffn_w_up, layer // 2)
            h, hb = _ffn_down(act, ffn_w_down_b, h, ln_ffn_g, ln_ffn_b, layer // 2, layer)
        else:
            h, hb = _moe_ffn(h, moe_router, moe_w_up, moe_w_down, layer // 2, ln_ffn_g, ln_ffn_b, layer)
    return h.reshape(B, S, D)
```

```python
import functools

import jax
import jax.numpy as jnp
from jax import lax
from jax.experimental import pallas as pl
from jax.experimental.pallas import tpu as pltpu

F32 = jnp.float32
BF16 = jnp.bfloat16

D_MODEL = 2048
DEPTH = 4
GRID_W = 64
ATTN_WIDTH = 1024
CONV_CH = 1024
HEAD_DIM = 128
N_Q_HEADS = 8
N_KV_HEADS = 2
GQA_GROUP = 4
KV_WIDTH = 256
QKV_COLS = ATTN_WIDTH + 2 * KV_WIDTH
ROPE_THETA = 10000.0
ROPE_AXIS_DIM = 64
CONV_K = 31
CONV_HALO = 16
FF_DIM = 5632
N_EXPERTS = 8
ALPHA = (2.0 * DEPTH) ** 0.25
LN_EPS = 1e-5
RMS_EPS = 1e-6
LOG2_E = 1.4426950408889634

V7X_VMEM_BYTES = 64 * 1024 * 1024
VMEM_LIMIT = V7X_VMEM_BYTES - 8 * 1024 * 1024
LANES = 128
SUBLANES = 8

TM = 512
TM_PROJ = 1024
TM_UP = 1024
TN = 512
TQ = 1024
TK = 1024
RB_ATTN = 32
TS_CONV = 512
RC_CONV = 128
TM_MOE = 512
MOE_WIDE_TILES = FF_DIM // (2 * TN)
TK_DOWN = 1408


def _params(sem, flags=None):
    return pltpu.CompilerParams(dimension_semantics=sem, vmem_limit_bytes=VMEM_LIMIT, flags=flags)


def _layer_norm(z, g, b):
    mu = jnp.mean(z, axis=-1, keepdims=True)
    zc = z - mu
    var = jnp.mean(zc * zc, axis=-1, keepdims=True)
    return zc * lax.rsqrt(var + LN_EPS) * g + b


def _ln_in_kernel(x_ref, g_ref, b_ref, h_ref, hb_ref):
    y = _layer_norm(x_ref[...], g_ref[...], b_ref[...])
    h_ref[...] = y
    hb_ref[...] = y.astype(BF16)


def _ln_in(x2, g, b):
    T, D = x2.shape
    row = pl.BlockSpec((TM, D), lambda i: (i, 0))
    vec = pl.BlockSpec((1, D), lambda i: (0, 0))
    return pl.pallas_call(
        _ln_in_kernel,
        grid=(T // TM,),
        in_specs=[row, vec, vec],
        out_specs=[row, row],
        out_shape=[jax.ShapeDtypeStruct((T, D), F32), jax.ShapeDtypeStruct((T, D), BF16)],
        compiler_params=_params(("parallel",)),
        name="ln_in",
    )(x2, g.reshape(1, D), b.reshape(1, D))


def _pipelined(n, n_tiles, work, finish, buf_a, buf_b, extra=True):
    for parity, (cur, prev) in enumerate(((buf_a, buf_b), (buf_b, buf_a))):
        @pl.when(jnp.logical_and(jnp.logical_and(n % 2 == parity, n < n_tiles), extra))
        def _(cur=cur, prev=prev):
            work(cur)
            finish(prev)

    last = buf_a if (n_tiles - 1) % 2 == 0 else buf_b

    @pl.when(jnp.logical_and(n == n_tiles, extra))
    def _():
        finish(last)


def _qkv_kernel(a_ref, w_ref, cos_ref, sin_ref, gain_ref, flag_ref, o_ref, wb_ref, acc_a, acc_b, *, n_rows, n_steps):
    n = pl.program_id(0)

    @pl.when(n == 0)
    def _():
        acc_b[...] = jnp.zeros(acc_b.shape, F32)

    @pl.when(jnp.logical_and(n % n_rows == 0, n < n_steps))
    def _():
        wb_ref[...] = w_ref[...].astype(BF16)

    lane = lax.broadcasted_iota(jnp.int32, (TM_PROJ, HEAD_DIM), 1)
    first_half = (lane & (ROPE_AXIS_DIM // 2)) == 0

    def work(acc_cur):
        acc_cur[...] = jnp.dot(a_ref[...], wb_ref[...], preferred_element_type=F32)

    def finish(acc_prev):
        cos = cos_ref[...]
        sin = sin_ref[...]
        for c in range(TN // HEAD_DIM):
            cs = slice(c * HEAD_DIM, (c + 1) * HEAD_DIM)
            on = flag_ref[:, cs] > 0.0
            x = acc_prev[:, cs]
            ms = jnp.mean(x * x, axis=-1, keepdims=True)
            xn = x * jnp.where(on, lax.rsqrt(ms + RMS_EPS), 1.0) * gain_ref[:, cs]
            sw = jnp.where(first_half, pltpu.roll(xn, HEAD_DIM - 32, 1), pltpu.roll(xn, 32, 1))
            o_ref[:, cs] = (xn * jnp.where(on, cos, 1.0) + sw * jnp.where(on, sin, 0.0)).astype(BF16)

    _pipelined(n, n_steps, work, finish, acc_a, acc_b)


def _qkv_proj(hb, w_in, layer, cos_t, sin_t, gain3, flag3, S):
    T, D = hb.shape
    assert TN == 2 * KV_WIDTH and ATTN_WIDTH % TN == 0 and S % TM_PROJ == 0
    s_tiles = S // TM_PROJ
    n_rows = T // TM_PROJ
    n_steps = (QKV_COLS // TN) * n_rows

    def cur(n):
        t = jnp.minimum(n, n_steps - 1)
        return t // n_rows, t % n_rows

    def prev(n):
        t = jnp.maximum(n - 1, 0)
        return t // n_rows, t % n_rows

    return pl.pallas_call(
        functools.partial(_qkv_kernel, n_rows=n_rows, n_steps=n_steps),
        grid=(n_steps + 1,),
        in_specs=[
            pl.BlockSpec((TM_PROJ, D), lambda n: (cur(n)[1], 0)),
            pl.BlockSpec((None, D, TN), lambda n: (layer, 0, cur(n)[0])),
            pl.BlockSpec((TM_PROJ, HEAD_DIM), lambda n: (prev(n)[1] % s_tiles, 0)),
            pl.BlockSpec((TM_PROJ, HEAD_DIM), lambda n: (prev(n)[1] % s_tiles, 0)),
            pl.BlockSpec((None, None, 1, TN), lambda n: (layer, prev(n)[0], 0, 0)),
            pl.BlockSpec((None, 1, TN), lambda n: (prev(n)[0], 0, 0)),
        ],
        out_specs=pl.BlockSpec((TM_PROJ, TN), lambda n: (prev(n)[1], prev(n)[0])),
        out_shape=jax.ShapeDtypeStruct((T, QKV_COLS), BF16),
        scratch_shapes=[pltpu.VMEM((D, TN), BF16), pltpu.VMEM((TM_PROJ, TN), F32), pltpu.VMEM((TM_PROJ, TN), F32)],
        compiler_params=_params(("arbitrary",)),
        name="qkv_proj",
    )(hb, w_in, cos_t, sin_t, gain3, flag3)


def _glu_kernel(a_ref, wa_ref, wg_ref, o_ref, wab_ref, wgb_ref):
    i = pl.program_id(1)

    @pl.when(i == 0)
    def _():
        wab_ref[...] = wa_ref[...].astype(BF16)
        wgb_ref[...] = wg_ref[...].astype(BF16)

    a = a_ref[...]
    val = jnp.dot(a, wab_ref[...], preferred_element_type=F32)
    gate = jnp.dot(a, wgb_ref[...], preferred_element_type=F32)
    o_ref[...] = val * jax.nn.sigmoid(gate)


def _glu_proj(hb, w_in, layer):
    T, D = hb.shape
    val0 = QKV_COLS // TN
    gate0 = (QKV_COLS + CONV_CH) // TN
    return pl.pallas_call(
        _glu_kernel,
        grid=(CONV_CH // TN, T // TM_PROJ),
        in_specs=[
            pl.BlockSpec((TM_PROJ, D), lambda j, i: (i, 0)),
            pl.BlockSpec((None, D, TN), lambda j, i: (layer, 0, val0 + j)),
            pl.BlockSpec((None, D, TN), lambda j, i: (layer, 0, gate0 + j)),
        ],
        out_specs=pl.BlockSpec((TM_PROJ, TN), lambda j, i: (i, j)),
        out_shape=jax.ShapeDtypeStruct((T, CONV_CH), F32),
        scratch_shapes=[pltpu.VMEM((D, TN), BF16), pltpu.VMEM((D, TN), BF16)],
        compiler_params=_params(("arbitrary", "arbitrary")),
        name="glu_proj",
    )(hb, w_in, w_in)


def _conv_kernel(cur_ref, prev_ref, next_ref, w_ref, cb_ref, g_ref, b_ref, o_ref, ext_ref, y_ref):
    s = pl.program_id(1)
    ts = cur_ref.shape[1]
    has_prev = (s > 0).astype(F32)
    has_next = (s < pl.num_programs(1) - 1).astype(F32)
    ext_ref[0:CONV_HALO, :] = prev_ref[0] * has_prev
    ext_ref[CONV_HALO:CONV_HALO + ts, :] = cur_ref[0]
    ext_ref[CONV_HALO + ts:, :] = next_ref[0] * has_next
    off = CONV_HALO - CONV_K // 2
    span = (off + CONV_K - 1) // SUBLANES * SUBLANES
    for c in range(CONV_CH // LANES):
        cs = slice(c * LANES, (c + 1) * LANES)
        for r in range(ts // RC_CONV):
            r0 = r * RC_CONV
            acc = jnp.broadcast_to(cb_ref[:, cs], (RC_CONV, LANES))
            for phase in range(SUBLANES):
                taps = [k for k in range(CONV_K) if (off + k) % SUBLANES == phase]
                shifted = ext_ref[r0 + phase:r0 + phase + RC_CONV + span, cs]
                part = None
                for k in taps:
                    a0 = off + k - phase
                    term = shifted[a0:a0 + RC_CONV, :] * w_ref[k:k + 1, cs]
                    part = term if part is None else part + term
                acc = acc + part
            y_ref[r0:r0 + RC_CONV, cs] = acc
    y = _layer_norm(y_ref[...], g_ref[...], b_ref[...])
    o_ref[0] = (y * jax.nn.sigmoid(y)).astype(BF16)


def _conv_module(u3, conv_w, conv_b, conv_g, conv_bt, layer):
    B, S, C = u3.shape
    ts = TS_CONV
    assert S % ts == 0 and ts % CONV_HALO == 0 and ts % RC_CONV == 0
    halo_per_tile = ts // CONV_HALO
    n_halo = S // CONV_HALO
    vec = pl.BlockSpec((None, 1, C), lambda b, s: (layer, 0, 0))
    return pl.pallas_call(
        _conv_kernel,
        grid=(B, S // ts),
        in_specs=[
            pl.BlockSpec((1, ts, C), lambda b, s: (b, s, 0)),
            pl.BlockSpec((1, CONV_HALO, C), lambda b, s: (b, jnp.maximum(s * halo_per_tile - 1, 0), 0)),
            pl.BlockSpec((1, CONV_HALO, C), lambda b, s: (b, jnp.minimum((s + 1) * halo_per_tile, n_halo - 1), 0)),
            pl.BlockSpec((None, CONV_K, C), lambda b, s: (layer, 0, 0)),
            vec, vec, vec,
        ],
        out_specs=pl.BlockSpec((1, ts, C), lambda b, s: (b, s, 0)),
        out_shape=jax.ShapeDtypeStruct((B, S, C), BF16),
        scratch_shapes=[pltpu.VMEM((ts + 2 * CONV_HALO, C), F32), pltpu.VMEM((ts, C), F32)],
        compiler_params=_params(("parallel", "parallel")),
        name="conv_module",
    )(u3, u3, u3, conv_w, conv_b, conv_g, conv_bt)


def _lane_tree(op, x):
    parts = [x[:, j * LANES:(j + 1) * LANES] for j in range(x.shape[1] // LANES)]
    while len(parts) > 1:
        parts = [op(parts[i], parts[i + 1]) for i in range(0, len(parts) - 1, 2)] + (parts[-1:] if len(parts) % 2 else [])
    return parts[0]


def _exact_zero(x):
    u = pltpu.bitcast(x, jnp.uint32)
    u = lax.shift_right_logical(lax.shift_right_logical(u, jnp.uint32(16)), jnp.uint32(16))
    return pltpu.bitcast(u, F32)


def _attn_kernel(q_ref, k_ref, v_ref, o_ref, sa_ref, sb_ref, ma_ref, mb_ref, p_ref, v1_ref):
    n = pl.program_id(0)
    S = k_ref.shape[1]
    n_groups = S // LANES

    @pl.when(n == 0)
    def _():
        sb_ref[...] = jnp.zeros(sb_ref.shape, F32)
        mb_ref[...] = jnp.zeros(mb_ref.shape, F32)

    @pl.when(jnp.maximum(n - 1, 0) % ((S // TQ) * GQA_GROUP) == 0)
    def _():
        v1_ref[:, 0:HEAD_DIM] = v_ref[0]
        v1_ref[:, HEAD_DIM:] = jnp.ones((S, HEAD_DIM), BF16)

    def step(s_cur, m_cur, s_prev, m_prev):
        q = q_ref[0]
        for c in range(S // TK):
            ks = slice(c * TK, (c + 1) * TK)
            s_cur[:, ks] = lax.dot_general(q, k_ref[0, ks, :], (((1,), (1,)), ((), ())), preferred_element_type=F32)
        for r0 in range(0, TQ, RB_ATTN):
            rs = slice(r0, r0 + RB_ATTN)
            t = s_cur[rs, 0:LANES]
            for j in range(1, n_groups):
                t = jnp.maximum(t, s_cur[rs, j * LANES:(j + 1) * LANES])
            m_cur[rs, :] = jnp.broadcast_to(jnp.max(t, axis=-1, keepdims=True), (RB_ATTN, LANES))
        for r0 in range(0, TQ, RB_ATTN):
            rs = slice(r0, r0 + RB_ATTN)
            m = m_prev[rs, :]
            for j in range(n_groups):
                js = slice(j * LANES, (j + 1) * LANES)
                p_ref[rs, js] = jnp.exp2(s_prev[rs, js] - m).astype(BF16)
        acc = jnp.dot(p_ref[...], v1_ref[...], preferred_element_type=F32)
        o_ref[0] = (acc[:, 0:HEAD_DIM] / acc[:, HEAD_DIM:]).astype(BF16)

    @pl.when(n % 2 == 0)
    def _():
        step(sa_ref, ma_ref, sb_ref, mb_ref)

    @pl.when(n % 2 == 1)
    def _():
        step(sb_ref, mb_ref, sa_ref, ma_ref)


def _attention(qkv3):
    B, S, _ = qkv3.shape
    assert S % TK == 0
    k0 = ATTN_WIDTH // HEAD_DIM
    v0 = (ATTN_WIDTH + KV_WIDTH) // HEAD_DIM
    nq = S // TQ
    n_tiles = B * N_Q_HEADS * nq

    def tile(t):
        return t // (nq * N_Q_HEADS), (t // nq) % N_Q_HEADS, t % nq

    def q_map(n):
        b, h, qi = tile(jnp.minimum(n, n_tiles - 1))
        return b, qi, h

    def k_map(n):
        b, h, _ = tile(jnp.minimum(n, n_tiles - 1))
        return b, 0, k0 + h // GQA_GROUP

    def v_map(n):
        b, h, _ = tile(jnp.maximum(n - 1, 0))
        return b, 0, v0 + h // GQA_GROUP

    def o_map(n):
        b, h, qi = tile(jnp.maximum(n - 1, 0))
        return b, qi, h

    return pl.pallas_call(
        _attn_kernel,
        grid=(n_tiles + 1,),
        in_specs=[
            pl.BlockSpec((1, TQ, HEAD_DIM), q_map),
            pl.BlockSpec((1, S, HEAD_DIM), k_map),
            pl.BlockSpec((1, S, HEAD_DIM), v_map),
        ],
        out_specs=pl.BlockSpec((1, TQ, HEAD_DIM), o_map),
        out_shape=jax.ShapeDtypeStruct((B, S, ATTN_WIDTH), BF16),
        scratch_shapes=[pltpu.VMEM((TQ, S), F32), pltpu.VMEM((TQ, S), F32),
                        pltpu.VMEM((TQ, LANES), F32), pltpu.VMEM((TQ, LANES), F32),
                        pltpu.VMEM((TQ, S), BF16), pltpu.VMEM((S, 2 * HEAD_DIM), BF16)],
        compiler_params=_params(("arbitrary",)),
        name="attention",
    )(qkv3, qkv3, qkv3)


def _out_proj_kernel(attn_ref, conv_ref, w_ref, h_ref, g_ref, b_ref, ho_ref, hbo_ref):
    y = jnp.dot(attn_ref[...], w_ref[0:ATTN_WIDTH, :], preferred_element_type=F32)
    y = y + jnp.dot(conv_ref[...], w_ref[ATTN_WIDTH:, :], preferred_element_type=F32)
    z = _layer_norm(ALPHA * h_ref[...] + y, g_ref[...], b_ref[...])
    ho_ref[...] = z
    hbo_ref[...] = z.astype(BF16)


def _out_proj(attn, conv, w_out_b, h, g, b, layer):
    T, D = h.shape
    row = pl.BlockSpec((TM, D), lambda i: (i, 0))
    half = pl.BlockSpec((TM, ATTN_WIDTH), lambda i: (i, 0))
    vec = pl.BlockSpec((None, 1, D), lambda i: (layer, 0, 0))
    return pl.pallas_call(
        _out_proj_kernel,
        grid=(T // TM,),
        in_specs=[half, half, pl.BlockSpec((None, D, D), lambda i: (layer, 0, 0)), row, vec, vec],
        out_specs=[row, row],
        out_shape=[jax.ShapeDtypeStruct((T, D), F32), jax.ShapeDtypeStruct((T, D), BF16)],
        compiler_params=_params(("parallel",)),
        name="out_proj",
    )(attn, conv, w_out_b, h, g, b)


def _ffn_up_kernel(a_ref, wg_ref, wu_ref, o_ref, wgb_ref, wub_ref):
    i = pl.program_id(1)

    @pl.when(i == 0)
    def _():
        wgb_ref[...] = wg_ref[...].astype(BF16)
        wub_ref[...] = wu_ref[...].astype(BF16)

    a = a_ref[...]
    g = jnp.dot(a, wgb_ref[...], preferred_element_type=F32)
    u = jnp.dot(a, wub_ref[...], preferred_element_type=F32)
    o_ref[...] = (g * jax.nn.sigmoid(g) * u).astype(BF16)


def _ffn_up(hb, w_up, layer):
    T, D = hb.shape
    nf = FF_DIM // TN
    return pl.pallas_call(
        _ffn_up_kernel,
        grid=(nf, T // TM_UP),
        in_specs=[
            pl.BlockSpec((TM_UP, D), lambda f, i: (i, 0)),
            pl.BlockSpec((None, D, TN), lambda f, i: (layer, 0, f)),
            pl.BlockSpec((None, D, TN), lambda f, i: (layer, 0, nf + f)),
        ],
        out_specs=pl.BlockSpec((TM_UP, TN), lambda f, i: (i, f)),
        out_shape=jax.ShapeDtypeStruct((T, FF_DIM), BF16),
        scratch_shapes=[pltpu.VMEM((D, TN), BF16), pltpu.VMEM((D, TN), BF16)],
        compiler_params=_params(("arbitrary", "arbitrary")),
        name="ffn_up",
    )(hb, w_up, w_up)


def _ffn_down_kernel(a_ref, w_ref, h_ref, g_ref, b_ref, ho_ref, hbo_ref, acc_ref):
    k = pl.program_id(1)
    part = jnp.dot(a_ref[...], w_ref[...], preferred_element_type=F32)

    @pl.when(k == 0)
    def _():
        acc_ref[...] = part

    @pl.when(k > 0)
    def _():
        acc_ref[...] += part

    @pl.when(k == pl.num_programs(1) - 1)
    def _():
        z = _layer_norm(ALPHA * h_ref[...] + acc_ref[...], g_ref[...], b_ref[...])
        ho_ref[...] = z
        hbo_ref[...] = z.astype(BF16)


def _ffn_down(act, w_down_b, h, g, b, dense_layer, layer):
    T, D = h.shape
    row = pl.BlockSpec((TM, D), lambda i, k: (i, 0))
    vec = pl.BlockSpec((None, 1, D), lambda i, k: (layer, 0, 0))
    return pl.pallas_call(
        _ffn_down_kernel,
        grid=(T // TM, FF_DIM // TK_DOWN),
        in_specs=[
            pl.BlockSpec((TM, TK_DOWN), lambda i, k: (i, k)),
            pl.BlockSpec((None, TK_DOWN, D), lambda i, k: (dense_layer, k, 0)),
            row, vec, vec,
        ],
        out_specs=[row, row],
        out_shape=[jax.ShapeDtypeStruct((T, D), F32), jax.ShapeDtypeStruct((T, D), BF16)],
        scratch_shapes=[pltpu.VMEM((TM, D), F32)],
        compiler_params=_params(("parallel", "arbitrary")),
        name="ffn_down",
    )(act, w_down_b, h, g, b)


_L_E1, _L_E2, _L_R1, _L_R2, _L_G1, _L_G2 = range(6)


def _split_bf16(x):
    hi = x.astype(BF16)
    return hi, (x - hi.astype(F32)).astype(BF16)


def _router_kernel(h_ref, w1_ref, w2_ref, info_ref, cnt_ref, carry_ref):
    i = pl.program_id(0)

    @pl.when(i == 0)
    def _():
        carry_ref[...] = jnp.zeros_like(carry_ref)

    hi, lo = _split_bf16(h_ref[...])
    d1 = jnp.dot(hi, w1_ref[...], preferred_element_type=F32)
    d2 = jnp.dot(lo, w2_ref[...], preferred_element_type=F32)
    logits = d1 + pltpu.roll(d1, LANES - N_EXPERTS, 1) + d2
    lane = lax.broadcasted_iota(jnp.int32, logits.shape, 1).astype(F32)
    neg = jnp.float32(-jnp.inf)
    lg = jnp.where(lane < N_EXPERTS, logits, neg)
    m1 = jnp.max(lg, axis=-1, keepdims=True)
    i1 = jnp.min(jnp.where(lg == m1, lane, float(LANES)), axis=-1, keepdims=True)
    oh1 = lane == i1
    lg2 = jnp.where(oh1, neg, lg)
    m2 = jnp.max(lg2, axis=-1, keepdims=True)
    i2 = jnp.min(jnp.where(lg2 == m2, lane, float(LANES)), axis=-1, keepdims=True)
    oh2 = lane == i2
    e = jnp.exp(m2 - m1)
    g1 = 1.0 / (1.0 + e)
    g2 = e * g1
    cnt = oh1.astype(F32) + oh2.astype(F32)
    tm = cnt.shape[0]
    row = lax.broadcasted_iota(jnp.int32, (tm, tm), 0)
    col = lax.broadcasted_iota(jnp.int32, (tm, tm), 1)
    strict_lower = jnp.where(col < row, 1.0, 0.0).astype(BF16)
    before = jnp.dot(strict_lower, cnt.astype(BF16), preferred_element_type=F32) + carry_ref[...]
    r1 = jnp.sum(jnp.where(oh1, before, 0.0), axis=-1, keepdims=True)
    r2 = jnp.sum(jnp.where(oh2, before, 0.0), axis=-1, keepdims=True)
    carry_ref[...] += jnp.sum(cnt, axis=0, keepdims=True)
    cnt_ref[...] = jnp.broadcast_to(carry_ref[...], cnt_ref.shape)
    info = jnp.zeros_like(logits)
    for idx, val in ((_L_E1, i1), (_L_E2, i2), (_L_R1, r1), (_L_R2, r2), (_L_G1, g1), (_L_G2, g2)):
        info = jnp.where(lane == idx, val, info)
    info_ref[...] = info


def _router(h, w_router):
    T, D = h.shape
    whi, wlo = _split_bf16(w_router)
    w1 = jnp.pad(jnp.concatenate([whi, wlo], axis=1), ((0, 0), (0, LANES - 2 * N_EXPERTS)))
    w2 = jnp.pad(whi, ((0, 0), (0, LANES - N_EXPERTS)))
    wspec = pl.BlockSpec((D, LANES), lambda i: (0, 0))
    return pl.pallas_call(
        _router_kernel,
        grid=(T // TM,),
        in_specs=[pl.BlockSpec((TM, D), lambda i: (i, 0)), wspec, wspec],
        out_specs=[pl.BlockSpec((TM, LANES), lambda i: (i, 0)), pl.BlockSpec((8, LANES), lambda i: (0, 0))],
        out_shape=[jax.ShapeDtypeStruct((T, LANES), F32), jax.ShapeDtypeStruct((8, LANES), F32)],
        scratch_shapes=[pltpu.VMEM((1, LANES), F32)],
        compiler_params=_params(("arbitrary",)),
        name="moe_router",
    )(h, w1, w2)


def _row_copy(src_hbm, src_row, dst, dst_row, sem):
    return pltpu.make_async_copy(src_hbm.at[pl.ds(src_row, 1), :], dst.at[pl.ds(dst_row, 1), :], sem)


def _dispatch_kernel(nused_ref, tok_ref, h_hbm, o_ref, buf_ref, sem):
    i = pl.program_id(0)
    rows = buf_ref.shape[0]

    @pl.when(i < nused_ref[0])
    def _():
        def issue(r2, c):
            for q in range(2):
                r = 2 * r2 + q
                _row_copy(h_hbm, tok_ref[0, 0, r], buf_ref, r, sem).start(priority=q)
            return c

        lax.fori_loop(0, rows // 2, issue, 0, unroll=4)

        def wait(r, c):
            _row_copy(h_hbm, 0, buf_ref, r, sem).wait()
            return c

        lax.fori_loop(0, rows, wait, 0, unroll=8)
        o_ref[...] = buf_ref[...].astype(BF16)

    @pl.when(i >= nused_ref[0])
    def _():
        o_ref[...] = jnp.zeros_like(o_ref)


def _dispatch(h, buf_tok3, nused):
    T, D = h.shape
    n_blk = buf_tok3.shape[0]
    grid_spec = pltpu.PrefetchScalarGridSpec(
        num_scalar_prefetch=1,
        grid=(n_blk,),
        in_specs=[
            pl.BlockSpec((1, 1, TM_MOE), lambda i, nu: (i, 0, 0), memory_space=pltpu.SMEM),
            pl.BlockSpec(memory_space=pl.ANY),
        ],
        out_specs=pl.BlockSpec((TM_MOE, D), lambda i, nu: (i, 0)),
        scratch_shapes=[pltpu.VMEM((TM_MOE, D), F32), pltpu.SemaphoreType.DMA(())],
    )
    return pl.pallas_call(
        _dispatch_kernel,
        grid_spec=grid_spec,
        out_shape=jax.ShapeDtypeStruct((n_blk * TM_MOE, D), BF16),
        compiler_params=_params(("arbitrary",)),
        name="moe_dispatch",
    )(nused, buf_tok3, h)


def _expert_changed(be_ref, i):
    return jnp.logical_or(i == 0, be_ref[i] != be_ref[jnp.maximum(i - 1, 0)])


def _moe_up_kernel(be_ref, nused_ref, x_ref, *refs, n_sub):
    w_refs, o_ref, wb_refs = refs[:2 * n_sub], refs[2 * n_sub], refs[2 * n_sub + 1:]
    i = pl.program_id(1)

    @pl.when(_expert_changed(be_ref, i))
    def _():
        for w_ref, wb_ref in zip(w_refs, wb_refs):
            wb_ref[...] = w_ref[...].astype(BF16)

    @pl.when(i < nused_ref[0])
    def _():
        x = x_ref[...]
        for s in range(n_sub):
            g = jnp.dot(x, wb_refs[s][...], preferred_element_type=F32)
            u = jnp.dot(x, wb_refs[n_sub + s][...], preferred_element_type=F32)
            o_ref[:, s * TN:(s + 1) * TN] = (g * jax.nn.sigmoid(g) * u).astype(BF16)

    @pl.when(i >= nused_ref[0])
    def _():
        o_ref[...] = jnp.zeros_like(o_ref)


def _moe_up(xs, w_up, moe_layer, blk_exp, nused, col0, n_col_tiles, n_sub):
    P, D = xs.shape
    n_blk = P // TM_MOE
    width = n_sub * TN
    assert col0 % TN == 0 and FF_DIM % TN == 0

    def w_spec(half, s):
        base = (half * FF_DIM + col0) // TN + s
        return pl.BlockSpec((None, None, D, TN), lambda f, i, be, nu: (moe_layer, be[i], 0, base + f * n_sub))

    grid_spec = pltpu.PrefetchScalarGridSpec(
        num_scalar_prefetch=2,
        grid=(n_col_tiles, n_blk),
        in_specs=[pl.BlockSpec((TM_MOE, D), lambda f, i, be, nu: (jnp.minimum(i, jnp.maximum(nu[0] - 1, 0)), 0))]
        + [w_spec(half, s) for half in range(2) for s in range(n_sub)],
        out_specs=pl.BlockSpec((TM_MOE, width), lambda f, i, be, nu: (i, f)),
        scratch_shapes=[pltpu.VMEM((D, TN), BF16)] * (2 * n_sub),
    )
    return pl.pallas_call(
        functools.partial(_moe_up_kernel, n_sub=n_sub),
        grid_spec=grid_spec,
        out_shape=jax.ShapeDtypeStruct((P, n_col_tiles * width), BF16),
        compiler_params=_params(("arbitrary", "arbitrary")),
        name="moe_up",
    )(blk_exp, nused, xs, *([w_up] * (2 * n_sub)))


def _moe_down_kernel(be_ref, nused_ref, a0_ref, a1_ref, w_ref, o_ref, wb_ref):
    i = pl.program_id(1)
    f0 = a0_ref.shape[1]

    @pl.when(_expert_changed(be_ref, i))
    def _():
        wb_ref[...] = w_ref[...].astype(BF16)

    @pl.when(i < nused_ref[0])
    def _():
        y = jnp.dot(a0_ref[...], wb_ref[0:f0, :], preferred_element_type=F32)
        o_ref[...] = y + jnp.dot(a1_ref[...], wb_ref[f0:, :], preferred_element_type=F32)

    @pl.when(i >= nused_ref[0])
    def _():
        o_ref[...] = jnp.zeros_like(o_ref)


def _moe_down(act0, act1, w_down, moe_layer, blk_exp, nused):
    P, F0 = act0.shape
    F1 = act1.shape[1]
    F = F0 + F1
    n_blk = P // TM_MOE

    def rows(width):
        return pl.BlockSpec((TM_MOE, width), lambda j, i, be, nu: (jnp.minimum(i, jnp.maximum(nu[0] - 1, 0)), 0))

    grid_spec = pltpu.PrefetchScalarGridSpec(
        num_scalar_prefetch=2,
        grid=(D_MODEL // TN, n_blk),
        in_specs=[
            rows(F0), rows(F1),
            pl.BlockSpec((None, None, F, TN), lambda j, i, be, nu: (moe_layer, be[i], 0, j)),
        ],
        out_specs=pl.BlockSpec((TM_MOE, TN), lambda j, i, be, nu: (i, j)),
        scratch_shapes=[pltpu.VMEM((F, TN), BF16)],
    )
    return pl.pallas_call(
        _moe_down_kernel,
        grid_spec=grid_spec,
        out_shape=jax.ShapeDtypeStruct((P, D_MODEL), F32),
        compiler_params=_params(("arbitrary", "arbitrary")),
        name="moe_down",
    )(blk_exp, nused, act0, act1, w_down)


def _combine_kernel(d1_ref, d2_ref, y_hbm, info_ref, h_ref, g_ref, b_ref, ho_ref, hbo_ref, y1_ref, y2_ref, sem):
    rows = y1_ref.shape[0]

    def issue(r, c):
        _row_copy(y_hbm, d1_ref[0, 0, r], y1_ref, r, sem.at[0]).start(priority=0)
        _row_copy(y_hbm, d2_ref[0, 0, r], y2_ref, r, sem.at[1]).start(priority=1)
        return c

    lax.fori_loop(0, rows, issue, 0, unroll=8)

    def wait(r, c):
        _row_copy(y_hbm, 0, y1_ref, r, sem.at[0]).wait()
        _row_copy(y_hbm, 0, y2_ref, r, sem.at[1]).wait()
        return c

    lax.fori_loop(0, rows, wait, 0, unroll=8)
    info = info_ref[...]
    g1 = info[:, _L_G1:_L_G1 + 1]
    g2 = info[:, _L_G2:_L_G2 + 1]
    y = y1_ref[...] * g1 + y2_ref[...] * g2
    z = _layer_norm(ALPHA * h_ref[...] + y, g_ref[...], b_ref[...])
    ho_ref[...] = z
    hbo_ref[...] = z.astype(BF16)


def _combine(y, dest1, dest2, info, h, g, b, layer):
    T, D = h.shape
    row = pl.BlockSpec((TM, D), lambda i: (i, 0))
    vec = pl.BlockSpec((None, 1, D), lambda i: (layer, 0, 0))
    idx = pl.BlockSpec((1, 1, TM), lambda i: (i, 0, 0), memory_space=pltpu.SMEM)
    return pl.pallas_call(
        _combine_kernel,
        grid=(T // TM,),
        in_specs=[idx, idx, pl.BlockSpec(memory_space=pl.ANY), pl.BlockSpec((TM, LANES), lambda i: (i, 0)), row, vec, vec],
        out_specs=[row, row],
        out_shape=[jax.ShapeDtypeStruct((T, D), F32), jax.ShapeDtypeStruct((T, D), BF16)],
        scratch_shapes=[pltpu.VMEM((TM, D), F32), pltpu.VMEM((TM, D), F32), pltpu.SemaphoreType.DMA((2,))],
        compiler_params=_params(("arbitrary",)),
        name="moe_combine",
    )(dest1.reshape(T // TM, 1, TM), dest2.reshape(T // TM, 1, TM), y, info, h, g, b)


def _moe_ffn(h, w_router, w_up, w_down, moe_layer, ln_g, ln_b, layer):
    T, D = h.shape
    n_blk = (2 * T) // TM_MOE + N_EXPERTS
    info, cnt = _router(h, w_router[moe_layer])
    counts = cnt[0, :N_EXPERTS].astype(jnp.int32)
    padded = (counts + TM_MOE - 1) // TM_MOE * TM_MOE
    pad_ends = jnp.cumsum(padded)
    pad_starts = pad_ends - padded
    e1 = info[:, _L_E1].astype(jnp.int32)
    e2 = info[:, _L_E2].astype(jnp.int32)
    dest1 = pad_starts[e1] + info[:, _L_R1].astype(jnp.int32)
    dest2 = pad_starts[e2] + info[:, _L_R2].astype(jnp.int32)
    tok = jnp.arange(T, dtype=jnp.int32)
    buf_tok = jnp.full((n_blk * TM_MOE,), T - 1, jnp.int32).at[jnp.concatenate([dest1, dest2])].set(
        jnp.concatenate([tok, tok]), unique_indices=True, mode="promise_in_bounds")
    nused = (pad_ends[-1] // TM_MOE).astype(jnp.int32).reshape(1)
    blk_start = jnp.minimum(jnp.arange(n_blk, dtype=jnp.int32), jnp.maximum(nused[0] - 1, 0)) * TM_MOE
    blk_exp = jnp.minimum(jnp.sum(blk_start[:, None] >= pad_ends[None, :], axis=1), N_EXPERTS - 1).astype(jnp.int32)
    xs = _dispatch(h, buf_tok.reshape(n_blk, 1, TM_MOE), nused)
    wide_cols = MOE_WIDE_TILES * 2 * TN
    act0 = _moe_up(xs, w_up, moe_layer, blk_exp, nused, 0, MOE_WIDE_TILES, 2)
    act1 = _moe_up(xs, w_up, moe_layer, blk_exp, nused, wide_cols, (FF_DIM - wide_cols) // TN, 1)
    y = _moe_down(act0, act1, w_down, moe_layer, blk_exp, nused)
    return _combine(y, dest1, dest2, info, h, ln_g, ln_b, layer)


def _rope_tables(S):
    t = jnp.arange(S, dtype=jnp.int32)
    inv_freq = ROPE_THETA ** (-jnp.arange(0, ROPE_AXIS_DIM, 2, dtype=F32) / ROPE_AXIS_DIM)
    ang_row = (t // GRID_W).astype(F32)[:, None] * inv_freq[None, :]
    ang_col = (t % GRID_W).astype(F32)[:, None] * inv_freq[None, :]
    cos = jnp.concatenate([jnp.cos(ang_row)] * 2 + [jnp.cos(ang_col)] * 2, axis=-1)
    sin = jnp.concatenate([-jnp.sin(ang_row), jnp.sin(ang_row), -jnp.sin(ang_col), jnp.sin(ang_col)], axis=-1)
    return cos, sin


def kernel(x, ln_in_g, ln_in_b, w_in, q_norm_g, k_norm_g, conv_w, conv_b, conv_ln_g, conv_ln_b, w_out, ln_mix_g, ln_mix_b, ffn_w_up, ffn_w_down, moe_router, moe_w_up, moe_w_down, ln_ffn_g, ln_ffn_b):
    B, S, D = x.shape
    T = B * S
    assert D == D_MODEL and T % TM_UP == 0 and S % TQ == 0
    cos_t, sin_t = _rope_tables(S)
    w_out_b = w_out.astype(BF16)
    ffn_w_down_b = ffn_w_down.astype(BF16)
    conv_b, conv_ln_g, conv_ln_b, ln_mix_g, ln_mix_b, ln_ffn_g, ln_ffn_b = (
        p.reshape(DEPTH, 1, -1)
        for p in (conv_b, conv_ln_g, conv_ln_b, ln_mix_g, ln_mix_b, ln_ffn_g, ln_ffn_b))
    heads_per_tile = TN // HEAD_DIM
    q_gain = jnp.tile(q_norm_g * (HEAD_DIM ** -0.5 * LOG2_E), (1, heads_per_tile))
    kv_gain = jnp.concatenate([jnp.tile(k_norm_g, (1, N_KV_HEADS)), jnp.ones((DEPTH, KV_WIDTH), F32)], axis=1)
    qkv_gain = jnp.stack([q_gain] * (ATTN_WIDTH // TN) + [kv_gain], axis=1).reshape(DEPTH, -1, 1, TN)
    kv_flag = jnp.concatenate([jnp.ones((KV_WIDTH,), F32), jnp.zeros((KV_WIDTH,), F32)])
    qkv_flag = jnp.stack([jnp.ones((TN,), F32)] * (ATTN_WIDTH // TN) + [kv_flag]).reshape(-1, 1, TN)
    h, hb = _ln_in(x.reshape(T, D), ln_in_g, ln_in_b)
    for layer in range(DEPTH):
        qkv = _qkv_proj(hb, w_in, layer, cos_t, sin_t, qkv_gain, qkv_flag, S)
        u = _glu_proj(hb, w_in, layer)
        attn = _attention(qkv.reshape(B, S, QKV_COLS)).reshape(T, ATTN_WIDTH)
        conv = _conv_module(u.reshape(B, S, CONV_CH), conv_w, conv_b, conv_ln_g, conv_ln_b, layer).reshape(T, CONV_CH)
        h, hb = _out_proj(attn, conv, w_out_b, h, ln_mix_g, ln_mix_b, layer)
        if layer % 2 == 0:
            act = _ffn_up(hb, ffn_w_up, layer // 2)
            h, hb = _ffn_down(act, ffn_w_down_b, h, ln_ffn_g, ln_ffn_b, layer // 2, layer)
        else:
            h, hb = _moe_ffn(h, moe_router, moe_w_up, moe_w_down, layer // 2, ln_ffn_g, ln_ffn_b, layer)
    return h.reshape(B, S, D)
```

```python
import functools

import jax
import jax.numpy as jnp
from jax import lax
from jax.experimental import pallas as pl
from jax.experimental.pallas import tpu as pltpu

F32 = jnp.float32
BF16 = jnp.bfloat16

D_MODEL = 2048
DEPTH = 4
GRID_W = 64
ATTN_WIDTH = 1024
CONV_CH = 1024
HEAD_DIM = 128
N_Q_HEADS = 8
N_KV_HEADS = 2
GQA_GROUP = 4
KV_WIDTH = 256
QKV_COLS = ATTN_WIDTH + 2 * KV_WIDTH
ROPE_THETA = 10000.0
ROPE_AXIS_DIM = 64
CONV_K = 31
CONV_HALO = 16
FF_DIM = 5632
N_EXPERTS = 8
ALPHA = (2.0 * DEPTH) ** 0.25
LN_EPS = 1e-5
RMS_EPS = 1e-6
LOG2_E = 1.4426950408889634

V7X_VMEM_BYTES = 64 * 1024 * 1024
VMEM_LIMIT = V7X_VMEM_BYTES - 8 * 1024 * 1024
LANES = 128
SUBLANES = 8

TM = 512
TM_PROJ = 1024
TM_UP = 1024
TN = 512
TQ = 1024
TK = 1024
RB_ATTN = 32
TS_CONV = 512
RC_CONV = 128
TM_MOE = 512
MOE_WIDE_TILES = FF_DIM // (2 * TN)
TK_DOWN = 1408


def _params(sem):
    return pltpu.CompilerParams(dimension_semantics=sem, vmem_limit_bytes=VMEM_LIMIT)


def _layer_norm(z, g, b):
    mu = jnp.mean(z, axis=-1, keepdims=True)
    zc = z - mu
    var = jnp.mean(zc * zc, axis=-1, keepdims=True)
    return zc * lax.rsqrt(var + LN_EPS) * g + b


def _ln_in_kernel(x_ref, g_ref, b_ref, h_ref, hb_ref):
    y = _layer_norm(x_ref[...], g_ref[...], b_ref[...])
    h_ref[...] = y
    hb_ref[...] = y.astype(BF16)


def _ln_in(x2, g, b):
    T, D = x2.shape
    row = pl.BlockSpec((TM, D), lambda i: (i, 0))
    vec = pl.BlockSpec((1, D), lambda i: (0, 0))
    return pl.pallas_call(
        _ln_in_kernel,
        grid=(T // TM,),
        in_specs=[row, vec, vec],
        out_specs=[row, row],
        out_shape=[jax.ShapeDtypeStruct((T, D), F32), jax.ShapeDtypeStruct((T, D), BF16)],
        compiler_params=_params(("parallel",)),
        name="ln_in",
    )(x2, g.reshape(1, D), b.reshape(1, D))


def _pipelined(n, n_tiles, work, finish, buf_a, buf_b, extra=True):
    for parity, (cur, prev) in enumerate(((buf_a, buf_b), (buf_b, buf_a))):
        @pl.when(jnp.logical_and(jnp.logical_and(n % 2 == parity, n < n_tiles), extra))
        def _(cur=cur, prev=prev):
            work(cur)
            finish(prev)

    last = buf_a if (n_tiles - 1) % 2 == 0 else buf_b

    @pl.when(jnp.logical_and(n == n_tiles, extra))
    def _():
        finish(last)


def _qkv_kernel(a_ref, w_ref, cos_ref, sin_ref, gain_ref, flag_ref, o_ref, wb_ref, acc_a, acc_b, *, n_rows, n_steps):
    n = pl.program_id(0)

    @pl.when(n == 0)
    def _():
        acc_b[...] = jnp.zeros(acc_b.shape, F32)

    @pl.when(jnp.logical_and(n % n_rows == 0, n < n_steps))
    def _():
        wb_ref[...] = w_ref[...].astype(BF16)

    lane = lax.broadcasted_iota(jnp.int32, (TM_PROJ, HEAD_DIM), 1)
    first_half = (lane & (ROPE_AXIS_DIM // 2)) == 0

    def work(acc_cur):
        acc_cur[...] = jnp.dot(a_ref[...], wb_ref[...], preferred_element_type=F32)

    def finish(acc_prev):
        cos = cos_ref[...]
        sin = sin_ref[...]
        for c in range(TN // HEAD_DIM):
            cs = slice(c * HEAD_DIM, (c + 1) * HEAD_DIM)
            on = flag_ref[:, cs] > 0.0
            x = acc_prev[:, cs]
            ms = jnp.mean(x * x, axis=-1, keepdims=True)
            xn = x * jnp.where(on, lax.rsqrt(ms + RMS_EPS), 1.0) * gain_ref[:, cs]
            sw = jnp.where(first_half, pltpu.roll(xn, HEAD_DIM - 32, 1), pltpu.roll(xn, 32, 1))
            o_ref[:, cs] = (xn * jnp.where(on, cos, 1.0) + sw * jnp.where(on, sin, 0.0)).astype(BF16)

    _pipelined(n, n_steps, work, finish, acc_a, acc_b)


def _qkv_proj(hb, w_in, layer, cos_t, sin_t, gain3, flag3, S):
    T, D = hb.shape
    assert TN == 2 * KV_WIDTH and ATTN_WIDTH % TN == 0 and S % TM_PROJ == 0
    s_tiles = S // TM_PROJ
    n_rows = T // TM_PROJ
    n_steps = (QKV_COLS // TN) * n_rows

    def cur(n):
        t = jnp.minimum(n, n_steps - 1)
        return t // n_rows, t % n_rows

    def prev(n):
        t = jnp.maximum(n - 1, 0)
        return t // n_rows, t % n_rows

    return pl.pallas_call(
        functools.partial(_qkv_kernel, n_rows=n_rows, n_steps=n_steps),
        grid=(n_steps + 1,),
        in_specs=[
            pl.BlockSpec((TM_PROJ, D), lambda n: (cur(n)[1], 0)),
            pl.BlockSpec((None, D, TN), lambda n: (layer, 0, cur(n)[0])),
            pl.BlockSpec((TM_PROJ, HEAD_DIM), lambda n: (prev(n)[1] % s_tiles, 0)),
            pl.BlockSpec((TM_PROJ, HEAD_DIM), lambda n: (prev(n)[1] % s_tiles, 0)),
            pl.BlockSpec((None, None, 1, TN), lambda n: (layer, prev(n)[0], 0, 0)),
            pl.BlockSpec((None, 1, TN), lambda n: (prev(n)[0], 0, 0)),
        ],
        out_specs=pl.BlockSpec((TM_PROJ, TN), lambda n: (prev(n)[1], prev(n)[0])),
        out_shape=jax.ShapeDtypeStruct((T, QKV_COLS), BF16),
        scratch_shapes=[pltpu.VMEM((D, TN), BF16), pltpu.VMEM((TM_PROJ, TN), F32), pltpu.VMEM((TM_PROJ, TN), F32)],
        compiler_params=_params(("arbitrary",)),
        name="qkv_proj",
    )(hb, w_in, cos_t, sin_t, gain3, flag3)


def _glu_kernel(a_ref, wa_ref, wg_ref, o_ref, wab_ref, wgb_ref):
    i = pl.program_id(1)

    @pl.when(i == 0)
    def _():
        wab_ref[...] = wa_ref[...].astype(BF16)
        wgb_ref[...] = wg_ref[...].astype(BF16)

    a = a_ref[...]
    val = jnp.dot(a, wab_ref[...], preferred_element_type=F32)
    gate = jnp.dot(a, wgb_ref[...], preferred_element_type=F32)
    o_ref[...] = val * jax.nn.sigmoid(gate)


def _glu_proj(hb, w_in, layer):
    T, D = hb.shape
    val0 = QKV_COLS // TN
    gate0 = (QKV_COLS + CONV_CH) // TN
    return pl.pallas_call(
        _glu_kernel,
        grid=(CONV_CH // TN, T // TM_PROJ),
        in_specs=[
            pl.BlockSpec((TM_PROJ, D), lambda j, i: (i, 0)),
            pl.BlockSpec((None, D, TN), lambda j, i: (layer, 0, val0 + j)),
            pl.BlockSpec((None, D, TN), lambda j, i: (layer, 0, gate0 + j)),
        ],
        out_specs=pl.BlockSpec((TM_PROJ, TN), lambda j, i: (i, j)),
        out_shape=jax.ShapeDtypeStruct((T, CONV_CH), F32),
        scratch_shapes=[pltpu.VMEM((D, TN), BF16), pltpu.VMEM((D, TN), BF16)],
        compiler_params=_params(("arbitrary", "arbitrary")),
        name="glu_proj",
    )(hb, w_in, w_in)


def _conv_kernel(cur_ref, prev_ref, next_ref, w_ref, cb_ref, g_ref, b_ref, o_ref, ext_ref, y_ref):
    s = pl.program_id(1)
    ts = cur_ref.shape[1]
    has_prev = (s > 0).astype(F32)
    has_next = (s < pl.num_programs(1) - 1).astype(F32)
    ext_ref[0:CONV_HALO, :] = prev_ref[0] * has_prev
    ext_ref[CONV_HALO:CONV_HALO + ts, :] = cur_ref[0]
    ext_ref[CONV_HALO + ts:, :] = next_ref[0] * has_next
    off = CONV_HALO - CONV_K // 2
    span = (off + CONV_K - 1) // SUBLANES * SUBLANES
    for c in range(CONV_CH // LANES):
        cs = slice(c * LANES, (c + 1) * LANES)
        for r in range(ts // RC_CONV):
            r0 = r * RC_CONV
            acc = jnp.broadcast_to(cb_ref[:, cs], (RC_CONV, LANES))
            for phase in range(SUBLANES):
                taps = [k for k in range(CONV_K) if (off + k) % SUBLANES == phase]
                shifted = ext_ref[r0 + phase:r0 + phase + RC_CONV + span, cs]
                part = None
                for k in taps:
                    a0 = off + k - phase
                    term = shifted[a0:a0 + RC_CONV, :] * w_ref[k:k + 1, cs]
                    part = term if part is None else part + term
                acc = acc + part
            y_ref[r0:r0 + RC_CONV, cs] = acc
    y = _layer_norm(y_ref[...], g_ref[...], b_ref[...])
    o_ref[0] = (y * jax.nn.sigmoid(y)).astype(BF16)


def _conv_module(u3, conv_w, conv_b, conv_g, conv_bt, layer):
    B, S, C = u3.shape
    ts = TS_CONV
    assert S % ts == 0 and ts % CONV_HALO == 0 and ts % RC_CONV == 0
    halo_per_tile = ts // CONV_HALO
    n_halo = S // CONV_HALO
    vec = pl.BlockSpec((None, 1, C), lambda b, s: (layer, 0, 0))
    return pl.pallas_call(
        _conv_kernel,
        grid=(B, S // ts),
        in_specs=[
            pl.BlockSpec((1, ts, C), lambda b, s: (b, s, 0)),
            pl.BlockSpec((1, CONV_HALO, C), lambda b, s: (b, jnp.maximum(s * halo_per_tile - 1, 0), 0)),
            pl.BlockSpec((1, CONV_HALO, C), lambda b, s: (b, jnp.minimum((s + 1) * halo_per_tile, n_halo - 1), 0)),
            pl.BlockSpec((None, CONV_K, C), lambda b, s: (layer, 0, 0)),
            vec, vec, vec,
        ],
        out_specs=pl.BlockSpec((1, ts, C), lambda b, s: (b, s, 0)),
        out_shape=jax.ShapeDtypeStruct((B, S, C), BF16),
        scratch_shapes=[pltpu.VMEM((ts + 2 * CONV_HALO, C), F32), pltpu.VMEM((ts, C), F32)],
        compiler_params=_params(("parallel", "parallel")),
        name="conv_module",
    )(u3, u3, u3, conv_w, conv_b, conv_g, conv_bt)


def _attn_kernel(q_ref, k_ref, v_ref, o_ref, sa_ref, sb_ref, ma_ref, mb_ref, p_ref, v1_ref):
    n = pl.program_id(0)
    S = k_ref.shape[1]
    n_groups = S // LANES

    @pl.when(n == 0)
    def _():
        sb_ref[...] = jnp.zeros(sb_ref.shape, F32)
        mb_ref[...] = jnp.zeros(mb_ref.shape, F32)

    @pl.when(jnp.maximum(n - 1, 0) % ((S // TQ) * GQA_GROUP) == 0)
    def _():
        v1_ref[:, 0:HEAD_DIM] = v_ref[0]
        v1_ref[:, HEAD_DIM:] = jnp.ones((S, HEAD_DIM), BF16)

    def step(s_cur, m_cur, s_prev, m_prev):
        q = q_ref[0]
        for c in range(S // TK):
            ks = slice(c * TK, (c + 1) * TK)
            s_cur[:, ks] = lax.dot_general(q, k_ref[0, ks, :], (((1,), (1,)), ((), ())), preferred_element_type=F32)
        for r0 in range(0, TQ, RB_ATTN):
            rs = slice(r0, r0 + RB_ATTN)
            t = s_cur[rs, 0:LANES]
            for j in range(1, n_groups):
                t = jnp.maximum(t, s_cur[rs, j * LANES:(j + 1) * LANES])
            m_cur[rs, :] = jnp.broadcast_to(jnp.max(t, axis=-1, keepdims=True), (RB_ATTN, LANES))
        for r0 in range(0, TQ, RB_ATTN):
            rs = slice(r0, r0 + RB_ATTN)
            m = m_prev[rs, :]
            for j in range(n_groups):
                js = slice(j * LANES, (j + 1) * LANES)
                p_ref[rs, js] = jnp.exp2(s_prev[rs, js] - m).astype(BF16)
        acc = jnp.dot(p_ref[...], v1_ref[...], preferred_element_type=F32)
        o_ref[0] = (acc[:, 0:HEAD_DIM] / acc[:, HEAD_DIM:]).astype(BF16)

    @pl.when(n % 2 == 0)
    def _():
        step(sa_ref, ma_ref, sb_ref, mb_ref)

    @pl.when(n % 2 == 1)
    def _():
        step(sb_ref, mb_ref, sa_ref, ma_ref)


def _attention(qkv3):
    B, S, _ = qkv3.shape
    assert S % TK == 0
    k0 = ATTN_WIDTH // HEAD_DIM
    v0 = (ATTN_WIDTH + KV_WIDTH) // HEAD_DIM
    nq = S // TQ
    n_tiles = B * N_Q_HEADS * nq

    def tile(t):
        return t // (nq * N_Q_HEADS), (t // nq) % N_Q_HEADS, t % nq

    def q_map(n):
        b, h, qi = tile(jnp.minimum(n, n_tiles - 1))
        return b, qi, h

    def k_map(n):
        b, h, _ = tile(jnp.minimum(n, n_tiles - 1))
        return b, 0, k0 + h // GQA_GROUP

    def v_map(n):
        b, h, _ = tile(jnp.maximum(n - 1, 0))
        return b, 0, v0 + h // GQA_GROUP

    def o_map(n):
        b, h, qi = tile(jnp.maximum(n - 1, 0))
        return b, qi, h

    return pl.pallas_call(
        _attn_kernel,
        grid=(n_tiles + 1,),
        in_specs=[
            pl.BlockSpec((1, TQ, HEAD_DIM), q_map),
            pl.BlockSpec((1, S, HEAD_DIM), k_map),
            pl.BlockSpec((1, S, HEAD_DIM), v_map),
        ],
        out_specs=pl.BlockSpec((1, TQ, HEAD_DIM), o_map),
        out_shape=jax.ShapeDtypeStruct((B, S, ATTN_WIDTH), BF16),
        scratch_shapes=[pltpu.VMEM((TQ, S), F32), pltpu.VMEM((TQ, S), F32),
                        pltpu.VMEM((TQ, LANES), F32), pltpu.VMEM((TQ, LANES), F32),
                        pltpu.VMEM((TQ, S), BF16), pltpu.VMEM((S, 2 * HEAD_DIM), BF16)],
        compiler_params=_params(("arbitrary",)),
        name="attention",
    )(qkv3, qkv3, qkv3)


def _out_proj_kernel(attn_ref, conv_ref, w_ref, h_ref, g_ref, b_ref, ho_ref, hbo_ref):
    y = jnp.dot(attn_ref[...], w_ref[0:ATTN_WIDTH, :], preferred_element_type=F32)
    y = y + jnp.dot(conv_ref[...], w_ref[ATTN_WIDTH:, :], preferred_element_type=F32)
    z = _layer_norm(ALPHA * h_ref[...] + y, g_ref[...], b_ref[...])
    ho_ref[...] = z
    hbo_ref[...] = z.astype(BF16)


def _out_proj(attn, conv, w_out_b, h, g, b, layer):
    T, D = h.shape
    row = pl.BlockSpec((TM, D), lambda i: (i, 0))
    half = pl.BlockSpec((TM, ATTN_WIDTH), lambda i: (i, 0))
    vec = pl.BlockSpec((None, 1, D), lambda i: (layer, 0, 0))
    return pl.pallas_call(
        _out_proj_kernel,
        grid=(T // TM,),
        in_specs=[half, half, pl.BlockSpec((None, D, D), lambda i: (layer, 0, 0)), row, vec, vec],
        out_specs=[row, row],
        out_shape=[jax.ShapeDtypeStruct((T, D), F32), jax.ShapeDtypeStruct((T, D), BF16)],
        compiler_params=_params(("parallel",)),
        name="out_proj",
    )(attn, conv, w_out_b, h, g, b)


def _ffn_up_kernel(a_ref, wg_ref, wu_ref, o_ref, wgb_ref, wub_ref):
    i = pl.program_id(1)

    @pl.when(i == 0)
    def _():
        wgb_ref[...] = wg_ref[...].astype(BF16)
        wub_ref[...] = wu_ref[...].astype(BF16)

    a = a_ref[...]
    g = jnp.dot(a, wgb_ref[...], preferred_element_type=F32)
    u = jnp.dot(a, wub_ref[...], preferred_element_type=F32)
    o_ref[...] = (g * jax.nn.sigmoid(g) * u).astype(BF16)


def _ffn_up(hb, w_up, layer):
    T, D = hb.shape
    nf = FF_DIM // TN
    return pl.pallas_call(
        _ffn_up_kernel,
        grid=(nf, T // TM_UP),
        in_specs=[
            pl.BlockSpec((TM_UP, D), lambda f, i: (i, 0)),
            pl.BlockSpec((None, D, TN), lambda f, i: (layer, 0, f)),
            pl.BlockSpec((None, D, TN), lambda f, i: (layer, 0, nf + f)),
        ],
        out_specs=pl.BlockSpec((TM_UP, TN), lambda f, i: (i, f)),
        out_shape=jax.ShapeDtypeStruct((T, FF_DIM), BF16),
        scratch_shapes=[pltpu.VMEM((D, TN), BF16), pltpu.VMEM((D, TN), BF16)],
        compiler_params=_params(("arbitrary", "arbitrary")),
        name="ffn_up",
    )(hb, w_up, w_up)


def _ffn_down_kernel(a_ref, w_ref, h_ref, g_ref, b_ref, ho_ref, hbo_ref, acc_ref):
    k = pl.program_id(1)
    part = jnp.dot(a_ref[...], w_ref[...], preferred_element_type=F32)

    @pl.when(k == 0)
    def _():
        acc_ref[...] = part

    @pl.when(k > 0)
    def _():
        acc_ref[...] += part

    @pl.when(k == pl.num_programs(1) - 1)
    def _():
        z = _layer_norm(ALPHA * h_ref[...] + acc_ref[...], g_ref[...], b_ref[...])
        ho_ref[...] = z
        hbo_ref[...] = z.astype(BF16)


def _ffn_down(act, w_down_b, h, g, b, dense_layer, layer):
    T, D = h.shape
    row = pl.BlockSpec((TM, D), lambda i, k: (i, 0))
    vec = pl.BlockSpec((None, 1, D), lambda i, k: (layer, 0, 0))
    return pl.pallas_call(
        _ffn_down_kernel,
        grid=(T // TM, FF_DIM // TK_DOWN),
        in_specs=[
            pl.BlockSpec((TM, TK_DOWN), lambda i, k: (i, k)),
            pl.BlockSpec((None, TK_DOWN, D), lambda i, k: (dense_layer, k, 0)),
            row, vec, vec,
        ],
        out_specs=[row, row],
        out_shape=[jax.ShapeDtypeStruct((T, D), F32), jax.ShapeDtypeStruct((T, D), BF16)],
        scratch_shapes=[pltpu.VMEM((TM, D), F32)],
        compiler_params=_params(("parallel", "arbitrary")),
        name="ffn_down",
    )(act, w_down_b, h, g, b)


_L_E1, _L_E2, _L_R1, _L_R2, _L_G1, _L_G2 = range(6)


def _split_bf16(x):
    hi = x.astype(BF16)
    return hi, (x - hi.astype(F32)).astype(BF16)


def _router_kernel(h_ref, w1_ref, w2_ref, info_ref, cnt_ref, carry_ref):
    i = pl.program_id(0)

    @pl.when(i == 0)
    def _():
        carry_ref[...] = jnp.zeros_like(carry_ref)

    hi, lo = _split_bf16(h_ref[...])
    d1 = jnp.dot(hi, w1_ref[...], preferred_element_type=F32)
    d2 = jnp.dot(lo, w2_ref[...], preferred_element_type=F32)
    logits = d1 + pltpu.roll(d1, LANES - N_EXPERTS, 1) + d2
    lane = lax.broadcasted_iota(jnp.int32, logits.shape, 1).astype(F32)
    neg = jnp.float32(-jnp.inf)
    lg = jnp.where(lane < N_EXPERTS, logits, neg)
    m1 = jnp.max(lg, axis=-1, keepdims=True)
    i1 = jnp.min(jnp.where(lg == m1, lane, float(LANES)), axis=-1, keepdims=True)
    oh1 = lane == i1
    lg2 = jnp.where(oh1, neg, lg)
    m2 = jnp.max(lg2, axis=-1, keepdims=True)
    i2 = jnp.min(jnp.where(lg2 == m2, lane, float(LANES)), axis=-1, keepdims=True)
    oh2 = lane == i2
    e = jnp.exp(m2 - m1)
    g1 = 1.0 / (1.0 + e)
    g2 = e * g1
    cnt = oh1.astype(F32) + oh2.astype(F32)
    tm = cnt.shape[0]
    row = lax.broadcasted_iota(jnp.int32, (tm, tm), 0)
    col = lax.broadcasted_iota(jnp.int32, (tm, tm), 1)
    strict_lower = jnp.where(col < row, 1.0, 0.0).astype(BF16)
    before = jnp.dot(strict_lower, cnt.astype(BF16), preferred_element_type=F32) + carry_ref[...]
    r1 = jnp.sum(jnp.where(oh1, before, 0.0), axis=-1, keepdims=True)
    r2 = jnp.sum(jnp.where(oh2, before, 0.0), axis=-1, keepdims=True)
    carry_ref[...] += jnp.sum(cnt, axis=0, keepdims=True)
    cnt_ref[...] = jnp.broadcast_to(carry_ref[...], cnt_ref.shape)
    info = jnp.zeros_like(logits)
    for idx, val in ((_L_E1, i1), (_L_E2, i2), (_L_R1, r1), (_L_R2, r2), (_L_G1, g1), (_L_G2, g2)):
        info = jnp.where(lane == idx, val, info)
    info_ref[...] = info


def _router(h, w_router):
    T, D = h.shape
    whi, wlo = _split_bf16(w_router)
    w1 = jnp.pad(jnp.concatenate([whi, wlo], axis=1), ((0, 0), (0, LANES - 2 * N_EXPERTS)))
    w2 = jnp.pad(whi, ((0, 0), (0, LANES - N_EXPERTS)))
    wspec = pl.BlockSpec((D, LANES), lambda i: (0, 0))
    return pl.pallas_call(
        _router_kernel,
        grid=(T // TM,),
        in_specs=[pl.BlockSpec((TM, D), lambda i: (i, 0)), wspec, wspec],
        out_specs=[pl.BlockSpec((TM, LANES), lambda i: (i, 0)), pl.BlockSpec((8, LANES), lambda i: (0, 0))],
        out_shape=[jax.ShapeDtypeStruct((T, LANES), F32), jax.ShapeDtypeStruct((8, LANES), F32)],
        scratch_shapes=[pltpu.VMEM((1, LANES), F32)],
        compiler_params=_params(("arbitrary",)),
        name="moe_router",
    )(h, w1, w2)


def _row_copy(src_hbm, src_row, dst, dst_row, sem):
    return pltpu.make_async_copy(src_hbm.at[pl.ds(src_row, 1), :], dst.at[pl.ds(dst_row, 1), :], sem)


def _dispatch_kernel(nused_ref, tok_ref, h_hbm, o_ref, buf_ref, sem):
    i = pl.program_id(0)
    rows = buf_ref.shape[0]

    @pl.when(i < nused_ref[0])
    def _():
        def issue(r, c):
            _row_copy(h_hbm, tok_ref[0, 0, r], buf_ref, r, sem).start()
            return c

        lax.fori_loop(0, rows, issue, 0, unroll=8)

        def wait(r, c):
            _row_copy(h_hbm, 0, buf_ref, r, sem).wait()
            return c

        lax.fori_loop(0, rows, wait, 0, unroll=8)
        o_ref[...] = buf_ref[...].astype(BF16)

    @pl.when(i >= nused_ref[0])
    def _():
        o_ref[...] = jnp.zeros_like(o_ref)


def _dispatch(h, buf_tok3, nused):
    T, D = h.shape
    n_blk = buf_tok3.shape[0]
    grid_spec = pltpu.PrefetchScalarGridSpec(
        num_scalar_prefetch=1,
        grid=(n_blk,),
        in_specs=[
            pl.BlockSpec((1, 1, TM_MOE), lambda i, nu: (i, 0, 0), memory_space=pltpu.SMEM),
            pl.BlockSpec(memory_space=pl.ANY),
        ],
        out_specs=pl.BlockSpec((TM_MOE, D), lambda i, nu: (i, 0)),
        scratch_shapes=[pltpu.VMEM((TM_MOE, D), F32), pltpu.SemaphoreType.DMA(())],
    )
    return pl.pallas_call(
        _dispatch_kernel,
        grid_spec=grid_spec,
        out_shape=jax.ShapeDtypeStruct((n_blk * TM_MOE, D), BF16),
        compiler_params=_params(("arbitrary",)),
        name="moe_dispatch",
    )(nused, buf_tok3, h)


def _expert_changed(be_ref, i):
    return jnp.logical_or(i == 0, be_ref[i] != be_ref[jnp.maximum(i - 1, 0)])


def _moe_up_kernel(be_ref, nused_ref, x_ref, *refs, n_sub):
    w_refs, o_ref, wb_refs = refs[:2 * n_sub], refs[2 * n_sub], refs[2 * n_sub + 1:]
    i = pl.program_id(1)

    @pl.when(_expert_changed(be_ref, i))
    def _():
        for w_ref, wb_ref in zip(w_refs, wb_refs):
            wb_ref[...] = w_ref[...].astype(BF16)

    @pl.when(i < nused_ref[0])
    def _():
        x = x_ref[...]
        for s in range(n_sub):
            g = jnp.dot(x, wb_refs[s][...], preferred_element_type=F32)
            u = jnp.dot(x, wb_refs[n_sub + s][...], preferred_element_type=F32)
            o_ref[:, s * TN:(s + 1) * TN] = (g * jax.nn.sigmoid(g) * u).astype(BF16)

    @pl.when(i >= nused_ref[0])
    def _():
        o_ref[...] = jnp.zeros_like(o_ref)


def _moe_up(xs, w_up, moe_layer, blk_exp, nused, col0, n_col_tiles, n_sub):
    P, D = xs.shape
    n_blk = P // TM_MOE
    width = n_sub * TN
    assert col0 % TN == 0 and FF_DIM % TN == 0

    def w_spec(half, s):
        base = (half * FF_DIM + col0) // TN + s
        return pl.BlockSpec((None, None, D, TN), lambda f, i, be, nu: (moe_layer, be[i], 0, base + f * n_sub))

    grid_spec = pltpu.PrefetchScalarGridSpec(
        num_scalar_prefetch=2,
        grid=(n_col_tiles, n_blk),
        in_specs=[pl.BlockSpec((TM_MOE, D), lambda f, i, be, nu: (jnp.minimum(i, jnp.maximum(nu[0] - 1, 0)), 0))]
        + [w_spec(half, s) for half in range(2) for s in range(n_sub)],
        out_specs=pl.BlockSpec((TM_MOE, width), lambda f, i, be, nu: (i, f)),
        scratch_shapes=[pltpu.VMEM((D, TN), BF16)] * (2 * n_sub),
    )
    return pl.pallas_call(
        functools.partial(_moe_up_kernel, n_sub=n_sub),
        grid_spec=grid_spec,
        out_shape=jax.ShapeDtypeStruct((P, n_col_tiles * width), BF16),
        compiler_params=_params(("arbitrary", "arbitrary")),
        name="moe_up",
    )(blk_exp, nused, xs, *([w_up] * (2 * n_sub)))


def _moe_down_kernel(be_ref, nused_ref, a0_ref, a1_ref, w_ref, o_ref, wb_ref):
    i = pl.program_id(1)
    f0 = a0_ref.shape[1]

    @pl.when(_expert_changed(be_ref, i))
    def _():
        wb_ref[...] = w_ref[...].astype(BF16)

    @pl.when(i < nused_ref[0])
    def _():
        y = jnp.dot(a0_ref[...], wb_ref[0:f0, :], preferred_element_type=F32)
        o_ref[...] = y + jnp.dot(a1_ref[...], wb_ref[f0:, :], preferred_element_type=F32)

    @pl.when(i >= nused_ref[0])
    def _():
        o_ref[...] = jnp.zeros_like(o_ref)


def _moe_down(act0, act1, w_down, moe_layer, blk_exp, nused):
    P, F0 = act0.shape
    F1 = act1.shape[1]
    F = F0 + F1
    n_blk = P // TM_MOE

    def rows(width):
        return pl.BlockSpec((TM_MOE, width), lambda j, i, be, nu: (jnp.minimum(i, jnp.maximum(nu[0] - 1, 0)), 0))

    grid_spec = pltpu.PrefetchScalarGridSpec(
        num_scalar_prefetch=2,
        grid=(D_MODEL // TN, n_blk),
        in_specs=[
            rows(F0), rows(F1),
            pl.BlockSpec((None, None, F, TN), lambda j, i, be, nu: (moe_layer, be[i], 0, j)),
        ],
        out_specs=pl.BlockSpec((TM_MOE, TN), lambda j, i, be, nu: (i, j)),
        scratch_shapes=[pltpu.VMEM((F, TN), BF16)],
    )
    return pl.pallas_call(
        _moe_down_kernel,
        grid_spec=grid_spec,
        out_shape=jax.ShapeDtypeStruct((P, D_MODEL), F32),
        compiler_params=_params(("arbitrary", "arbitrary")),
        name="moe_down",
    )(blk_exp, nused, act0, act1, w_down)


def _combine_kernel(d1_ref, d2_ref, y_hbm, info_ref, h_ref, g_ref, b_ref, ho_ref, hbo_ref, y1_ref, y2_ref, sem):
    rows = y1_ref.shape[0]

    def issue(r, c):
        _row_copy(y_hbm, d1_ref[0, 0, r], y1_ref, r, sem.at[0]).start()
        _row_copy(y_hbm, d2_ref[0, 0, r], y2_ref, r, sem.at[1]).start()
        return c

    lax.fori_loop(0, rows, issue, 0, unroll=8)

    def wait(r, c):
        _row_copy(y_hbm, 0, y1_ref, r, sem.at[0]).wait()
        _row_copy(y_hbm, 0, y2_ref, r, sem.at[1]).wait()
        return c

    lax.fori_loop(0, rows, wait, 0, unroll=8)
    info = info_ref[...]
    g1 = info[:, _L_G1:_L_G1 + 1]
    g2 = info[:, _L_G2:_L_G2 + 1]
    y = y1_ref[...] * g1 + y2_ref[...] * g2
    z = _layer_norm(ALPHA * h_ref[...] + y, g_ref[...], b_ref[...])
    ho_ref[...] = z
    hbo_ref[...] = z.astype(BF16)


def _combine(y, dest1, dest2, info, h, g, b, layer):
    T, D = h.shape
    row = pl.BlockSpec((TM, D), lambda i: (i, 0))
    vec = pl.BlockSpec((None, 1, D), lambda i: (layer, 0, 0))
    idx = pl.BlockSpec((1, 1, TM), lambda i: (i, 0, 0), memory_space=pltpu.SMEM)
    return pl.pallas_call(
        _combine_kernel,
        grid=(T // TM,),
        in_specs=[idx, idx, pl.BlockSpec(memory_space=pl.ANY), pl.BlockSpec((TM, LANES), lambda i: (i, 0)), row, vec, vec],
        out_specs=[row, row],
        out_shape=[jax.ShapeDtypeStruct((T, D), F32), jax.ShapeDtypeStruct((T, D), BF16)],
        scratch_shapes=[pltpu.VMEM((TM, D), F32), pltpu.VMEM((TM, D), F32), pltpu.SemaphoreType.DMA((2,))],
        compiler_params=_params(("arbitrary",)),
        name="moe_combine",
    )(dest1.reshape(T // TM, 1, TM), dest2.reshape(T // TM, 1, TM), y, info, h, g, b)


def _moe_ffn(h, w_router, w_up, w_down, moe_layer, ln_g, ln_b, layer):
    T, D = h.shape
    n_blk = (2 * T) // TM_MOE + N_EXPERTS
    info, cnt = _router(h, w_router[moe_layer])
    counts = cnt[0, :N_EXPERTS].astype(jnp.int32)
    padded = (counts + TM_MOE - 1) // TM_MOE * TM_MOE
    pad_ends = jnp.cumsum(padded)
    pad_starts = pad_ends - padded
    e1 = info[:, _L_E1].astype(jnp.int32)
    e2 = info[:, _L_E2].astype(jnp.int32)
    dest1 = pad_starts[e1] + info[:, _L_R1].astype(jnp.int32)
    dest2 = pad_starts[e2] + info[:, _L_R2].astype(jnp.int32)
    tok = jnp.arange(T, dtype=jnp.int32)
    buf_tok = jnp.full((n_blk * TM_MOE,), T - 1, jnp.int32).at[jnp.concatenate([dest1, dest2])].set(
        jnp.concatenate([tok, tok]), unique_indices=True, mode="promise_in_bounds")
    nused = (pad_ends[-1] // TM_MOE).astype(jnp.int32).reshape(1)
    blk_start = jnp.minimum(jnp.arange(n_blk, dtype=jnp.int32), jnp.maximum(nused[0] - 1, 0)) * TM_MOE
    blk_exp = jnp.minimum(jnp.sum(blk_start[:, None] >= pad_ends[None, :], axis=1), N_EXPERTS - 1).astype(jnp.int32)
    xs = _dispatch(h, buf_tok.reshape(n_blk, 1, TM_MOE), nused)
    wide_cols = MOE_WIDE_TILES * 2 * TN
    act0 = _moe_up(xs, w_up, moe_layer, blk_exp, nused, 0, MOE_WIDE_TILES, 2)
    act1 = _moe_up(xs, w_up, moe_layer, blk_exp, nused, wide_cols, (FF_DIM - wide_cols) // TN, 1)
    y = _moe_down(act0, act1, w_down, moe_layer, blk_exp, nused)
    return _combine(y, dest1, dest2, info, h, ln_g, ln_b, layer)


def _rope_tables(S):
    t = jnp.arange(S, dtype=jnp.int32)
    inv_freq = ROPE_THETA ** (-jnp.arange(0, ROPE_AXIS_DIM, 2, dtype=F32) / ROPE_AXIS_DIM)
    ang_row = (t // GRID_W).astype(F32)[:, None] * inv_freq[None, :]
    ang_col = (t % GRID_W).astype(F32)[:, None] * inv_freq[None, :]
    cos = jnp.concatenate([jnp.cos(ang_row)] * 2 + [jnp.cos(ang_col)] * 2, axis=-1)
    sin = jnp.concatenate([-jnp.sin(ang_row), jnp.sin(ang_row), -jnp.sin(ang_col), jnp.sin(ang_col)], axis=-1)
    return cos, sin


def kernel(x, ln_in_g, ln_in_b, w_in, q_norm_g, k_norm_g, conv_w, conv_b, conv_ln_g, conv_ln_b, w_out, ln_mix_g, ln_mix_b, ffn_w_up, ffn_w_down, moe_router, moe_w_up, moe_w_down, ln_ffn_g, ln_ffn_b):
    B, S, D = x.shape
    T = B * S
    assert D == D_MODEL and T % TM_UP == 0 and S % TQ == 0
    cos_t, sin_t = _rope_tables(S)
    w_out_b = w_out.astype(BF16)
    ffn_w_down_b = ffn_w_down.astype(BF16)
    conv_b, conv_ln_g, conv_ln_b, ln_mix_g, ln_mix_b, ln_ffn_g, ln_ffn_b = (
        p.reshape(DEPTH, 1, -1)
        for p in (conv_b, conv_ln_g, conv_ln_b, ln_mix_g, ln_mix_b, ln_ffn_g, ln_ffn_b))
    heads_per_tile = TN // HEAD_DIM
    q_gain = jnp.tile(q_norm_g * (HEAD_DIM ** -0.5 * LOG2_E), (1, heads_per_tile))
    kv_gain = jnp.concatenate([jnp.tile(k_norm_g, (1, N_KV_HEADS)), jnp.ones((DEPTH, KV_WIDTH), F32)], axis=1)
    qkv_gain = jnp.stack([q_gain] * (ATTN_WIDTH // TN) + [kv_gain], axis=1).reshape(DEPTH, -1, 1, TN)
    kv_flag = jnp.concatenate([jnp.ones((KV_WIDTH,), F32), jnp.zeros((KV_WIDTH,), F32)])
    qkv_flag = jnp.stack([jnp.ones((TN,), F32)] * (ATTN_WIDTH // TN) + [kv_flag]).reshape(-1, 1, TN)
    h, hb = _ln_in(x.reshape(T, D), ln_in_g, ln_in_b)
    for layer in range(DEPTH):
        qkv = _qkv_proj(hb, w_in, layer, cos_t, sin_t, qkv_gain, qkv_flag, S)
        u = _glu_proj(hb, w_in, layer)
        attn = _attention(qkv.reshape(B, S, QKV_COLS)).reshape(T, ATTN_WIDTH)
        conv = _conv_module(u.reshape(B, S, CONV_CH), conv_w, conv_b, conv_ln_g, conv_ln_b, layer).reshape(T, CONV_CH)
        h, hb = _out_proj(attn, conv, w_out_b, h, ln_mix_g, ln_mix_b, layer)
        if layer % 2 == 0:
            act = _ffn_up(hb, ffn_w_up, layer // 2)
            h, hb = _ffn_down(act, ffn_w_down_b, h, ln_ffn_g, ln_ffn_b, layer // 2, layer)
        else:
            h, hb = _moe_ffn(h, moe_router, moe_w_up, moe_w_down, layer // 2, ln_ffn_g, ln_ffn_b, layer)
    return h.reshape(B, S, D)
```
